```python
import math
import jax
import jax.numpy as jnp
from jax import lax
import numpy as np

D_MODEL = 1024
BATCH = 2
SEQ = 16384
DEPTH = 2

GRID_W = 64
CTX_LEN = 256
CHUNK = 64
BRANCH_W = 512
N_BRANCH = 4

GLA_HEADS = 4
GLA_DK = 64
GLA_DV = 128
GLA_GATE_RANK = 16
GLA_GATE_NORM = 16.0
ROPE_BASE = 10000.0

LRU_WIDTH = 512
LRU_BLOCKS = 8
LRU_BW = LRU_WIDTH // LRU_BLOCKS
LRU_CONV = 4
LRU_C = 8.0

GDN_HEADS = 4
GDN_DK = 128
GDN_DV = 128
GDN_CONV = 4

NA_HEADS = 8
NA_HD = 64
NA_WIN_R = 8
NA_WIN_C = 16

D_FF = 2816
FFN_CONV = 3

LN_EPS = 1e-5
NORM_EPS = 1e-6
ALPHA = (2 * DEPTH) ** 0.25
BETA = (8 * DEPTH) ** -0.25

IN_LAYOUT = (
    ('gla_q', GLA_HEADS * GLA_DK), ('gla_k', GLA_HEADS * GLA_DK), ('gla_v', GLA_HEADS * GLA_DV),
    ('gla_g', GLA_HEADS * GLA_DV), ('gla_fw', GLA_GATE_RANK), ('gla_bw', GLA_GATE_RANK),
    ('lru_x', LRU_WIDTH), ('lru_y', LRU_WIDTH),
    ('gdn_q', GDN_HEADS * GDN_DK), ('gdn_k', GDN_HEADS * GDN_DK), ('gdn_v', GDN_HEADS * GDN_DV),
    ('gdn_z', GDN_HEADS * GDN_DV), ('gdn_a', 2 * GDN_HEADS), ('gdn_b', 2 * GDN_HEADS),
    ('na_q', NA_HEADS * NA_HD), ('na_k', NA_HEADS * NA_HD), ('na_v', NA_HEADS * NA_HD),
    ('merge', N_BRANCH * D_MODEL),
)
IN_NAMES = tuple(n for n, _ in IN_LAYOUT)
IN_SIZES = tuple(s for _, s in IN_LAYOUT)
IN_WIDTH = sum(IN_SIZES)

kernel_name = 'hybrid_flow_backbone_gla_rglru_gdn_natten'


def split_columns(p):
    offsets = [int(o) for o in np.cumsum(IN_SIZES)[:-1]]
    return dict(zip(IN_NAMES, jnp.split(p, offsets, axis=-1)))


def layer_norm(x, g, b):
    xf = x.astype(jnp.float32)
    mu = jnp.mean(xf, -1, keepdims=True)
    var = jnp.mean(jnp.square(xf - mu), -1, keepdims=True)
    return ((xf - mu) * lax.rsqrt(var + LN_EPS)).astype(x.dtype) * g + b


def head_rms_norm(o, g):
    of = o.astype(jnp.float32)
    return of * lax.rsqrt(jnp.mean(jnp.square(of), -1, keepdims=True) + NORM_EPS) * g


def l2_norm(x):
    xf = x.astype(jnp.float32)
    return xf * lax.rsqrt(jnp.sum(jnp.square(xf), -1, keepdims=True) + NORM_EPS)


def to_heads(x, h):
    b, t, w = x.shape
    return x.reshape(b, t, h, w // h).transpose(0, 2, 1, 3)


def from_heads(x):
    b, h, t, d = x.shape
    return x.transpose(0, 2, 1, 3).reshape(b, t, h * d)


def flip_t(x):
    return jnp.flip(x, axis=2)


def dwconv(x, w, left):
    k, ch = w.shape
    return lax.conv_general_dilated(x, w[:, None, :].astype(x.dtype), (1,), ((left, k - 1 - left),),
                                    dimension_numbers=('NWC', 'WIO', 'NWC'), feature_group_count=ch)


def axial_rope(x):
    t_len, d = x.shape[2], x.shape[3]
    half = d // 2
    quarter = half // 2
    t = jnp.arange(t_len)
    inv = ROPE_BASE ** (-jnp.arange(quarter, dtype=jnp.float32) / quarter)
    ang = jnp.concatenate([(t // GRID_W).astype(jnp.float32)[:, None] * inv,
                           (t % GRID_W).astype(jnp.float32)[:, None] * inv], -1)
    cos, sin = jnp.cos(ang).astype(x.dtype), jnp.sin(ang).astype(x.dtype)
    x1, x2 = x[..., :half], x[..., half:]
    return jnp.concatenate([x1 * cos - x2 * sin, x1 * sin + x2 * cos], -1)


def causal_mask_incl():
    idx = jnp.arange(CHUNK)
    return idx[:, None] >= idx[None, :]


def gla_scan(q, k, v, log_a, s0, with_output):
    b_sz, h, t_len, dk = k.shape
    dv = v.shape[-1]
    n = t_len // CHUNK
    def chunks(a):
        return a.reshape(b_sz, h, n, CHUNK, a.shape[-1]).astype(jnp.float32)
    k, v, la = chunks(k), chunks(v), chunks(log_a)
    cum = jnp.cumsum(la, axis=3)
    cum_last = cum[:, :, :, -1:, :]
    u_loc = jnp.einsum('bhncd,bhnce->bhnde', k * jnp.exp(cum_last - cum), v)
    dec = jnp.exp(cum_last[:, :, :, 0, :])
    def step(s, xs):
        d_n, u_n = xs
        return s * d_n[..., None] + u_n, s
    s_final, s_prev = lax.scan(step, s0, (jnp.moveaxis(dec, 2, 0), jnp.moveaxis(u_loc, 2, 0)))
    if not with_output:
        return None, s_final
    q = chunks(q)
    cum_mid = cum[:, :, :, CHUNK // 2 - 1:CHUNK // 2, :]
    scores = jnp.einsum('bhncd,bhnsd->bhncs', q * jnp.exp(cum - cum_mid), k * jnp.exp(cum_mid - cum))
    scores = jnp.where(causal_mask_incl(), scores, 0.0)
    o = (jnp.einsum('bhncs,bhnse->bhnce', scores, v)
         + jnp.einsum('bhncd,nbhde->bhnce', q * jnp.exp(cum), s_prev))
    return o.reshape(b_sz, h, t_len, dv), s_final


def gdn_scan(q, k, v, g, beta, s0, with_output):
    b_sz, h, t_len, dk = k.shape
    dv = v.shape[-1]
    n = t_len // CHUNK
    def chunks(a):
        return a.reshape((b_sz, h, n, CHUNK) + a.shape[3:]).astype(jnp.float32)
    k, v, g, beta = chunks(k), chunks(v), chunks(g), chunks(beta)
    gc = jnp.cumsum(g, axis=-1)
    idx = jnp.arange(CHUNK)
    incl = idx[:, None] >= idx[None, :]
    strict = idx[:, None] > idx[None, :]
    decay = jnp.exp(jnp.where(incl, gc[..., :, None] - gc[..., None, :], -jnp.inf))
    kb = k * beta[..., None]
    vb = v * beta[..., None]
    lower = jnp.where(strict, jnp.einsum('bhnid,bhnjd->bhnij', kb, k) * decay, 0.0)
    eye = jnp.eye(CHUNK, dtype=jnp.float32)
    rhs = jnp.concatenate([vb, kb * jnp.exp(gc)[..., None]], axis=-1)
    sol = lax.linalg.triangular_solve(eye + lower, rhs, left_side=True, lower=True, unit_diagonal=True)
    u_c, w_c = sol[..., :dv], sol[..., dv:]
    k_dec = k * jnp.exp(gc[..., -1:] - gc)[..., None]
    chunk_dec = jnp.exp(gc[..., -1])
    mv = lambda a: jnp.moveaxis(a, 2, 0)
    if not with_output:
        def step_state(s, xs):
            u_n, w_n, kd_n, cd_n = xs
            v_new = u_n - jnp.einsum('bhck,bhkv->bhcv', w_n, s)
            return s * cd_n[..., None, None] + jnp.einsum('bhck,bhcv->bhkv', kd_n, v_new), None
        s_final, _ = lax.scan(step_state, s0, (mv(u_c), mv(w_c), mv(k_dec), mv(chunk_dec)))
        return None, s_final
    q = chunks(q) * (dk ** -0.5)
    attn = jnp.where(incl, jnp.einsum('bhnid,bhnjd->bhnij', q, k) * decay, 0.0)
    q_dec = q * jnp.exp(gc)[..., None]
    def step(s, xs):
        u_n, w_n, kd_n, cd_n, a_n, qd_n = xs
        v_new = u_n - jnp.einsum('bhck,bhkv->bhcv', w_n, s)
        o_n = jnp.einsum('bhck,bhkv->bhcv', qd_n, s) + jnp.einsum('bhcs,bhsv->bhcv', a_n, v_new)
        s = s * cd_n[..., None, None] + jnp.einsum('bhck,bhcv->bhkv', kd_n, v_new)
        return s, o_n
    s_final, o = lax.scan(step, s0, (mv(u_c), mv(w_c), mv(k_dec), mv(chunk_dec), mv(attn), mv(q_dec)))
    return jnp.moveaxis(o, 0, 2).reshape(b_sz, h, t_len, dv), s_final


def linear_scan(a, bx, h0):
    def combine(e1, e2):
        a1, b1 = e1
        a2, b2 = e2
        return a1 * a2, a2 * b1 + b2
    a_cum, h = lax.associative_scan(combine, (a, bx), axis=1)
    return h + a_cum * h0[:, None, :]


def gla_mixer(P, Pc, w_gate, b_gate, norm_g, need_ctx):
    def prep(Q, rope):
        q = to_heads(Q['gla_q'], GLA_HEADS) * (GLA_DK ** -0.5)
        k = to_heads(Q['gla_k'], GLA_HEADS)
        if rope:
            q, k = axial_rope(q), axial_rope(k)
        v = to_heads(Q['gla_v'], GLA_HEADS)
        la_f = to_heads(jax.nn.log_sigmoid(Q['gla_fw'] @ w_gate[0] + b_gate[0]) / GLA_GATE_NORM, GLA_HEADS)
        la_b = to_heads(jax.nn.log_sigmoid(Q['gla_bw'] @ w_gate[1] + b_gate[1]) / GLA_GATE_NORM, GLA_HEADS)
        return q, k, v, la_f, la_b
    b_sz = P['gla_q'].shape[0]
    s_zero = jnp.zeros((b_sz, GLA_HEADS, GLA_DK, GLA_DV), jnp.float32)
    qc, kc, vc, lcf, lcb = prep(Pc, False)
    oc_f, sc_f = gla_scan(qc, kc, vc, lcf, s_zero, need_ctx)
    oc_b, sc_b = gla_scan(flip_t(qc), flip_t(kc), flip_t(vc), flip_t(lcb), s_zero, need_ctx)
    q, k, v, lf, lb = prep(P, True)
    o_f, _ = gla_scan(q, k, v, lf, sc_f, True)
    o_b, _ = gla_scan(flip_t(q), flip_t(k), flip_t(v), flip_t(lb), sc_b, True)
    def out(o, gate):
        return from_heads(head_rms_norm(o, norm_g)).astype(gate.dtype) * jax.nn.silu(gate)
    y = out(o_f + flip_t(o_b), P['gla_g'])
    yc = out(oc_f + flip_t(oc_b), Pc['gla_g']) if need_ctx else None
    return y, yc


def lru_mixer(P, Pc, conv_w, conv_b, gate_w, gate_b, lam, need_ctx):
    def scan_dir(xs, d, h0):
        b_sz, t_len, _ = xs.shape
        gt = jnp.einsum('btnd,gnde->gbtne', xs.reshape(b_sz, t_len, LRU_BLOCKS, LRU_BW), gate_w[d])
        gt = (gt.reshape(2, b_sz, t_len, LRU_WIDTH) + gate_b[d][:, None, None, :]).astype(jnp.float32)
        r, i = jax.nn.sigmoid(gt[0]), jax.nn.sigmoid(gt[1])
        log_a = -LRU_C * r * jax.nn.softplus(-lam[d].astype(jnp.float32))
        bx = jnp.sqrt(-jnp.expm1(2.0 * log_a)) * i * xs.astype(jnp.float32)
        return linear_scan(jnp.exp(log_a), bx, h0)
    flip1 = lambda a: jnp.flip(a, axis=1)
    xc = dwconv(Pc['lru_x'], conv_w, LRU_CONV // 2) + conv_b
    h_zero = jnp.zeros((xc.shape[0], LRU_WIDTH), jnp.float32)
    hc_f = scan_dir(xc, 0, h_zero)
    hc_b = flip1(scan_dir(flip1(xc), 1, h_zero))
    xl = dwconv(P['lru_x'], conv_w, LRU_CONV // 2) + conv_b
    h_f = scan_dir(xl, 0, hc_f[:, -1])
    h_b = flip1(scan_dir(flip1(xl), 1, hc_b[:, 0]))
    y = (h_f + h_b).astype(xl.dtype) * jax.nn.gelu(P['lru_y'])
    yc = (hc_f + hc_b).astype(xc.dtype) * jax.nn.gelu(Pc['lru_y']) if need_ctx else None
    return y, yc


def gdn_mixer(P, Pc, conv_w, a_log, dt_bias, norm_g, need_ctx):
    def prep(Q):
        qkv = jnp.concatenate([Q['gdn_q'], Q['gdn_k'], Q['gdn_v']], -1)
        qkv = jax.nn.silu(dwconv(qkv, conv_w, GDN_CONV // 2))
        q, k, v = jnp.split(qkv, [GDN_HEADS * GDN_DK, 2 * GDN_HEADS * GDN_DK], axis=-1)
        q, k, v = l2_norm(to_heads(q, GDN_HEADS)), l2_norm(to_heads(k, GDN_HEADS)), to_heads(v, GDN_HEADS)
        b_sz, t_len, _ = Q['gdn_a'].shape
        a = Q['gdn_a'].reshape(b_sz, t_len, 2, GDN_HEADS).transpose(2, 0, 3, 1).astype(jnp.float32)
        bb = Q['gdn_b'].reshape(b_sz, t_len, 2, GDN_HEADS).transpose(2, 0, 3, 1).astype(jnp.float32)
        g = -jnp.exp(a_log.astype(jnp.float32))[:, None, :, None] * jax.nn.softplus(
            a + dt_bias.astype(jnp.float32)[:, None, :, None])
        return q, k, v, g, jax.nn.sigmoid(bb)
    b_sz = P['gdn_q'].shape[0]
    s_zero = jnp.zeros((b_sz, GDN_HEADS, GDN_DK, GDN_DV), jnp.float32)
    qc, kc, vc, gcv, bc = prep(Pc)
    oc_f, sc_f = gdn_scan(qc, kc, vc, gcv[0], bc[0], s_zero, need_ctx)
    oc_b, sc_b = gdn_scan(flip_t(qc), flip_t(kc), flip_t(vc), flip_t(gcv[1]), flip_t(bc[1]), s_zero, need_ctx)
    q, k, v, g, beta = prep(P)
    o_f, _ = gdn_scan(q, k, v, g[0], beta[0], sc_f, True)
    o_b, _ = gdn_scan(flip_t(q), flip_t(k), flip_t(v), flip_t(g[1]), flip_t(beta[1]), sc_b, True)
    def out(o, z):
        return from_heads(head_rms_norm(o, norm_g)).astype(z.dtype) * jax.nn.silu(z)
    y = out(o_f + flip_t(o_b), P['gdn_z'])
    yc = out(oc_f + flip_t(oc_b), Pc['gdn_z']) if need_ctx else None
    return y, yc


def neighborhood_attention(q, k, v, k_ctx, v_ctx, rpb):
    b_sz, h, t_len, hd = q.shape
    rows = t_len // GRID_W
    wr = min(NA_WIN_R, rows)
    wc = NA_WIN_C
    n_loc = wr * wc
    col = jnp.arange(GRID_W)
    cs = jnp.clip(col - wc // 2, 0, GRID_W - wc)
    ii = jnp.arange(wr)
    jj = jnp.arange(wc)
    kidx = (ii[None, :, None] * GRID_W + cs[:, None, None] + jj[None, None, :]).reshape(GRID_W, n_loc)
    dc = cs[:, None] + jj[None, :] - col[:, None]
    q_rows = jnp.moveaxis(q.reshape(b_sz, h, rows, GRID_W, hd), 2, 0) * (hd ** -0.5)
    def one_row(args):
        r, q_r = args
        rs = jnp.clip(r - wr // 2, 0, rows - wr)
        k_win = jnp.take(lax.dynamic_slice_in_dim(k, rs * GRID_W, wr * GRID_W, axis=2), kidx, axis=2)
        v_win = jnp.take(lax.dynamic_slice_in_dim(v, rs * GRID_W, wr * GRID_W, axis=2), kidx, axis=2)
        dr = rs + ii - r
        bias = rpb[:, dr[:, None, None] + NA_WIN_R - 1, dc[None] + NA_WIN_C - 1]
        bias = bias.transpose(0, 2, 1, 3).reshape(h, GRID_W, n_loc)
        s_loc = jnp.einsum('bhqd,bhqkd->bhqk', q_r, k_win) + bias
        s_ctx = jnp.einsum('bhqd,bhkd->bhqk', q_r, k_ctx)
        p = jax.nn.softmax(jnp.concatenate([s_loc, s_ctx], -1).astype(jnp.float32), axis=-1).astype(v.dtype)
        return (jnp.einsum('bhqk,bhqkd->bhqd', p[..., :n_loc], v_win)
                + jnp.einsum('bhqk,bhkd->bhqd', p[..., n_loc:], v_ctx))
    o = lax.map(one_row, (jnp.arange(rows), q_rows))
    return jnp.moveaxis(o, 0, 2).reshape(b_sz, h, t_len, hd)


def context_attention(q, k, v):
    s = jnp.einsum('bhqd,bhkd->bhqk', q, k) * (q.shape[-1] ** -0.5)
    p = jax.nn.softmax(s.astype(jnp.float32), axis=-1).astype(v.dtype)
    return jnp.einsum('bhqk,bhkd->bhqd', p, v)


def na_mixer(P, Pc, rpb, need_ctx):
    q, k, v = (to_heads(P[n], NA_HEADS) for n in ('na_q', 'na_k', 'na_v'))
    qc, kc, vc = (to_heads(Pc[n], NA_HEADS) for n in ('na_q', 'na_k', 'na_v'))
    y = from_heads(neighborhood_attention(q, k, v, kc, vc, rpb))
    yc = from_heads(context_attention(qc, kc, vc)) if need_ctx else None
    return y, yc


def merge_branches(Q, ys, w_branch, w_out, b_out):
    gates = jax.nn.sigmoid(Q['merge'])
    m = gates[..., :D_MODEL] * (ys[0] @ w_branch[0])
    for i in range(1, N_BRANCH):
        m = m + gates[..., i * D_MODEL:(i + 1) * D_MODEL] * (ys[i] @ w_branch[i])
    return m @ w_out + b_out


def mixer_sublayer(u, uc, w_in, b_in, gla_w_gate, gla_b_gate, gla_norm, lru_conv_w, lru_conv_b,
                   lru_gate_w, lru_gate_b, lru_lambda, gdn_conv_w, gdn_a_log, gdn_dt_bias, gdn_norm,
                   na_rpb, w_branch, w_out, b_out, need_ctx):
    P = split_columns(u @ w_in + b_in)
    Pc = split_columns(uc @ w_in + b_in)
    y_a, yc_a = gla_mixer(P, Pc, gla_w_gate, gla_b_gate, gla_norm, need_ctx)
    y_b, yc_b = lru_mixer(P, Pc, lru_conv_w, lru_conv_b, lru_gate_w, lru_gate_b, lru_lambda, need_ctx)
    y_c, yc_c = gdn_mixer(P, Pc, gdn_conv_w, gdn_a_log, gdn_dt_bias, gdn_norm, need_ctx)
    y_d, yc_d = na_mixer(P, Pc, na_rpb, need_ctx)
    out = merge_branches(P, (y_a, y_b, y_c, y_d), w_branch, w_out, b_out)
    out_c = merge_branches(Pc, (yc_a, yc_b, yc_c, yc_d), w_branch, w_out, b_out) if need_ctx else None
    return out, out_c


def conv_ffn(u, w_up, b_up, conv_w, conv_b, w_down, b_down):
    a, b = jnp.split(u @ w_up + b_up, 2, axis=-1)
    a = dwconv(a, conv_w, FFN_CONV // 2) + conv_b
    return (jax.nn.silu(a) * b) @ w_down + b_down


def setup_inputs(seed: int = 0) -> dict:
    key = jax.random.key(seed)
    keys = jax.random.split(key, 48)
    f32 = jnp.float32
    L, D = DEPTH, D_MODEL
    def nrm(i, shape, scale):
        return jax.random.normal(keys[i], shape, f32) * scale
    def unif(i, shape, lo, hi):
        return jax.random.uniform(keys[i], shape, f32, lo, hi)
    lam_u = unif(16, (L, 2, LRU_WIDTH), 0.9, 0.999)
    dt = jnp.exp(unif(19, (L, 2, GDN_HEADS), math.log(1e-3), math.log(1e-1)))
    return {
        'x': nrm(0, (BATCH, SEQ, D), 1.0),
        'c': nrm(1, (BATCH, D), 1.0),
        'ctx': nrm(2, (BATCH, CTX_LEN, D), 1.0),
        'c_ctx': nrm(3, (D,), 1.0),
        'w_mod': nrm(4, (L, D, 6 * D), D ** -0.5),
        'b_mod': nrm(5, (L, 6 * D), 0.02),
        'w_in': nrm(6, (L, D, IN_WIDTH), D ** -0.5),
        'b_in': nrm(7, (L, IN_WIDTH), 0.02),
        'gla_w_gate': nrm(8, (L, 2, GLA_GATE_RANK, GLA_HEADS * GLA_DK), GLA_GATE_RANK ** -0.5),
        'gla_b_gate': nrm(9, (L, 2, GLA_HEADS * GLA_DK), 0.1),
        'gla_norm': 1.0 + nrm(10, (L, GLA_DV), 0.02),
        'lru_conv_w': nrm(11, (L, LRU_CONV, LRU_WIDTH), LRU_CONV ** -0.5),
        'lru_conv_b': nrm(12, (L, LRU_WIDTH), 0.02),
        'lru_gate_w': nrm(13, (L, 2, 2, LRU_BLOCKS, LRU_BW, LRU_BW), LRU_BW ** -0.5),
        'lru_gate_b': nrm(14, (L, 2, 2, LRU_WIDTH), 0.02),
        'lru_lambda': jnp.log(lam_u) - jnp.log1p(-lam_u),
        'gdn_conv_w': nrm(17, (L, GDN_CONV, 3 * GDN_HEADS * GDN_DK), GDN_CONV ** -0.5),
        'gdn_a_log': jnp.log(unif(18, (L, 2, GDN_HEADS), 1.0, 16.0)),
        'gdn_dt_bias': dt + jnp.log(-jnp.expm1(-dt)),
        'gdn_norm': 1.0 + nrm(20, (L, GDN_DV), 0.02),
        'na_rpb': nrm(21, (L, NA_HEADS, 2 * NA_WIN_R - 1, 2 * NA_WIN_C - 1), 0.1),
        'w_branch': nrm(22, (L, N_BRANCH, BRANCH_W, D), BRANCH_W ** -0.5),
        'w_out': nrm(23, (L, D, D), BETA * D ** -0.5),
        'b_out': nrm(24, (L, D), 0.02),
        'ln1_g': 1.0 + nrm(25, (L, D), 0.02),
        'ln1_b': nrm(26, (L, D), 0.02),
        'ffn_w_up': nrm(27, (L, D, 2 * D_FF), D ** -0.5),
        'ffn_b_up': nrm(28, (L, 2 * D_FF), 0.02),
        'ffn_conv_w': nrm(29, (L, FFN_CONV, D_FF), FFN_CONV ** -0.5),
        'ffn_conv_b': nrm(30, (L, D_FF), 0.02),
        'ffn_w_down': nrm(31, (L, D_FF, D), BETA * D_FF ** -0.5),
        'ffn_b_down': nrm(32, (L, D), 0.02),
        'ln2_g': 1.0 + nrm(33, (L, D), 0.02),
        'ln2_b': nrm(34, (L, D), 0.02),
    }


def reference(x, c, ctx, c_ctx, w_mod, b_mod, w_in, b_in, gla_w_gate, gla_b_gate, gla_norm,
              lru_conv_w, lru_conv_b, lru_gate_w, lru_gate_b, lru_lambda, gdn_conv_w, gdn_a_log,
              gdn_dt_bias, gdn_norm, na_rpb, w_branch, w_out, b_out, ln1_g, ln1_b, ffn_w_up, ffn_b_up,
              ffn_conv_w, ffn_conv_b, ffn_w_down, ffn_b_down, ln2_g, ln2_b):
    xc = ctx
    for l in range(DEPTH):
        need_ctx = l < DEPTH - 1
        mod = jax.nn.silu(c) @ w_mod[l] + b_mod[l]
        mod_c = jax.nn.silu(c_ctx) @ w_mod[l] + b_mod[l]
        sh1, sc1, g1, sh2, sc2, g2 = jnp.split(mod[:, None, :], 6, axis=-1)
        csh1, csc1, cg1, csh2, csc2, cg2 = jnp.split(mod_c, 6, axis=-1)
        u = x * (1.0 + sc1) + sh1
        uc = xc * (1.0 + csc1) + csh1
        out, out_c = mixer_sublayer(u, uc, w_in[l], b_in[l], gla_w_gate[l], gla_b_gate[l], gla_norm[l],
                                    lru_conv_w[l], lru_conv_b[l], lru_gate_w[l], lru_gate_b[l], lru_lambda[l],
                                    gdn_conv_w[l], gdn_a_log[l], gdn_dt_bias[l], gdn_norm[l], na_rpb[l],
                                    w_branch[l], w_out[l], b_out[l], need_ctx)
        x = layer_norm(ALPHA * x + g1 * out, ln1_g[l], ln1_b[l])
        f = conv_ffn(x * (1.0 + sc2) + sh2, ffn_w_up[l], ffn_b_up[l], ffn_conv_w[l], ffn_conv_b[l],
                     ffn_w_down[l], ffn_b_down[l])
        x = layer_norm(ALPHA * x + g2 * f, ln2_g[l], ln2_b[l])
        if need_ctx:
            xc = layer_norm(ALPHA * xc + cg1 * out_c, ln1_g[l], ln1_b[l])
            fc = conv_ffn(xc * (1.0 + csc2) + csh2, ffn_w_up[l], ffn_b_up[l], ffn_conv_w[l], ffn_conv_b[l],
                          ffn_w_down[l], ffn_b_down[l])
            xc = layer_norm(ALPHA * xc + cg2 * fc, ln2_g[l], ln2_b[l])
    return x
```

```python
import functools
import math

import jax
import jax.numpy as jnp
import numpy as np
from jax import lax
from jax.experimental import pallas as pl
from jax.experimental.pallas import tpu as pltpu

F32 = jnp.float32
BF16 = jnp.bfloat16

GRID_W = 64
CHUNK = 64
GLA_HEADS, GLA_DK, GLA_DV = 4, 64, 128
GLA_GATE_RANK = 16
GLA_GATE_NORM = 16.0
ROPE_BASE = 10000.0
LRU_WIDTH, LRU_BLOCKS, LRU_CONV, LRU_C = 512, 8, 4, 8.0
GDN_HEADS, GDN_DK, GDN_DV, GDN_CONV = 4, 128, 128, 4
NA_HEADS, NA_HD, NA_WIN_R, NA_WIN_C = 8, 64, 8, 16
FFN_CONV = 3
LN_EPS = 1e-5
NORM_EPS = 1e-6
BRANCH_W = 512

HALO = 16
TN = 512
NA_ROWS = 4
VMEM_LIMIT = 56 * 1024 * 1024
NEG_BIG = -1e30

T_GLA_QK, T_GLA_V, T_GLA_G = 0, 1, 2
T_GDN_Q, T_GDN_K, T_GDN_V, T_GDN_Z = 3, 4, 5, 6
T_LRU_X, T_LRU_Y = 7, 8
T_NA_Q, T_NA_K, T_NA_V = 9, 10, 11
T_MERGE = 12
T_SMALL = 20
N_TILES = 21
PLAIN_TILES = (T_GLA_V, T_GLA_G, T_GDN_Z, T_LRU_Y, T_NA_Q, T_NA_K, T_NA_V, T_SMALL)
SM_FW, SM_BW, SM_A, SM_B = 0, 16, 32, 40


def _cparams(sem):
    return pltpu.CompilerParams(dimension_semantics=sem, vmem_limit_bytes=VMEM_LIMIT)


def _mm(a, b):
    return jnp.dot(a, b, preferred_element_type=F32)


def _mm_nt(a, b):
    return lax.dot_general(a, b, (((1,), (1,)), ((), ())), preferred_element_type=F32)


def _mm_tn(a, b):
    return lax.dot_general(a, b, (((0,), (0,)), ((), ())), preferred_element_type=F32)


def _split2(a):
    hi = a.astype(BF16)
    lo = (a - hi.astype(F32)).astype(BF16)
    return hi, lo


def _mm3(a, b):
    ah, al = _split2(a)
    bh, bl = _split2(b)
    return _mm(ah, bh) + (_mm(ah, bl) + _mm(al, bh))


def _cumdot(tri_bf, x):
    h1 = x.astype(BF16)
    r1 = x - h1.astype(F32)
    h2 = r1.astype(BF16)
    h3 = (r1 - h2.astype(F32)).astype(BF16)
    return _mm(tri_bf, h1) + (_mm(tri_bf, h2) + _mm(tri_bf, h3))


def _sigmoid(x):
    return 1.0 / (1.0 + jnp.exp(-x))


def _silu(x):
    return x * _sigmoid(x)


def _softplus(x):
    return jnp.maximum(x, 0.0) + jnp.log1p(jnp.exp(-jnp.abs(x)))


def _gelu_tanh(x):
    return x * (0.5 * (1.0 + jnp.tanh(math.sqrt(2.0 / math.pi) * (x + 0.044715 * (x * x * x)))))


def _layer_norm(y, g, b):
    mu = jnp.mean(y, axis=-1, keepdims=True)
    yc = y - mu
    var = jnp.mean(yc * yc, axis=-1, keepdims=True)
    return yc * lax.rsqrt(var + LN_EPS) * g + b


def _row_pos(n, off):
    return lax.broadcasted_iota(jnp.int32, (n, 1), 0) + off


def _mod_kernel(c_ref, w_ref, b_ref, o_ref):
    cv = c_ref[...]
    o_ref[0] = jnp.dot(_silu(cv), w_ref[0], precision=lax.Precision.HIGHEST,
                       preferred_element_type=F32) + b_ref[0]


def _mod_call(crows, w_mod, b_mod):
    depth, d, n = w_mod.shape
    tn = n // 4
    return pl.pallas_call(
        _mod_kernel,
        out_shape=jax.ShapeDtypeStruct((depth, 8, n), F32),
        grid=(depth, n // tn),
        in_specs=[pl.BlockSpec((8, d), lambda l, j: (0, 0)),
                  pl.BlockSpec((1, d, tn), lambda l, j: (l, 0, j)),
                  pl.BlockSpec((1, 1, tn), lambda l, j: (l, 0, j))],
        out_specs=pl.BlockSpec((1, 8, tn), lambda l, j: (l, 0, j)),
        compiler_params=_cparams(("arbitrary", "arbitrary")),
        name="mod",
    )(crows, w_mod, b_mod.reshape(depth, 1, n))


def _build_lhs(lhs_ref, x_ref, xp_ref, xn_ref, scl, shl, scc, shc, p0, tm, ctx_len):
    def mod(xv, p):
        is_ctx = (p >= 0) & (p < ctx_len)
        sc = jnp.where(is_ctx, scc, scl)
        sh = jnp.where(is_ctx, shc, shl)
        return (xv * (1.0 + sc) + sh).astype(BF16)
    lhs_ref[0:HALO, :] = mod(xp_ref[...], _row_pos(HALO, p0 - HALO))
    lhs_ref[HALO:HALO + tm, :] = mod(x_ref[...], _row_pos(tm, p0))
    lhs_ref[HALO + tm:, :] = mod(xn_ref[...], _row_pos(HALO, p0 + tm))


def _dwconv(acc_ref, cw_ref, p0, tm, ctx_len, tb, left):
    p = _row_pos(tm, p0)
    in_ctx = p < ctx_len
    seg_lo = jnp.where(in_ctx, 0, ctx_len)
    seg_hi = jnp.where(in_ctx, ctx_len, tb)
    out = None
    for k in range(cw_ref.shape[0]):
        d = k - left
        ok = (p + d >= seg_lo) & (p + d < seg_hi)
        term = jnp.where(ok, acc_ref[pl.ds(HALO + d, tm), :], 0.0) * cw_ref[k:k + 1, :]
        out = term if out is None else out + term
    return out


def _halo_specs(tm, d, r):
    hb = tm // HALO
    return [pl.BlockSpec((tm, d), lambda i, j: (i, 0)),
            pl.BlockSpec((HALO, d), lambda i, j: (jnp.maximum(i * hb - 1, 0), 0)),
            pl.BlockSpec((HALO, d), lambda i, j: (jnp.minimum((i + 1) * hb, r // HALO - 1), 0))]


def _swap_half_heads(x):
    n = x.shape[1]
    lane = lax.broadcasted_iota(jnp.int32, x.shape, 1)
    return jnp.where((lane % 64) < 32, pltpu.roll(x, n - 32, 1), pltpu.roll(x, 32, 1))


def _l2_heads(x, width, scale):
    outs = []
    for h in range(x.shape[1] // width):
        xs = x[:, h * width:(h + 1) * width]
        ss = jnp.sum(xs * xs, axis=-1, keepdims=True)
        y = xs * lax.rsqrt(ss + NORM_EPS)
        outs.append(y * scale if scale != 1.0 else y)
    return jnp.concatenate(outs, axis=1)


def _proj_kernel(x_ref, xp_ref, xn_ref, scl_ref, shl_ref, scc_ref, shc_ref, w_ref, b_ref, cw_ref, cb_ref,
                 cos_ref, sin_ref, o_ref, lhs_ref, acc_ref, *, tm, tpb, ctx_len, tb):
    i = pl.program_id(0)
    j = pl.program_id(1)
    p0 = (i % tpb) * tm

    @pl.when(j == 0)
    def _():
        _build_lhs(lhs_ref, x_ref, xp_ref, xn_ref, scl_ref[0], shl_ref[0], scc_ref[...], shc_ref[...],
                   p0, tm, ctx_len)

    def plain():
        return _mm(lhs_ref[HALO:HALO + tm, :], w_ref[...]) + b_ref[...]

    def conv():
        acc_ref[...] = _mm(lhs_ref[...], w_ref[...]) + b_ref[...]
        return _dwconv(acc_ref, cw_ref, p0, tm, ctx_len, tb, LRU_CONV // 2) + cb_ref[...]

    @pl.when(j == T_GLA_QK)
    def _():
        a = plain()
        reps = a.shape[1] // cos_ref.shape[1]
        cos = jnp.concatenate([cos_ref[...]] * reps, axis=1)
        sin = jnp.concatenate([sin_ref[...]] * reps, axis=1)
        o_ref[...] = a * cos + _swap_half_heads(a) * sin

    @pl.when(functools.reduce(jnp.logical_or, [j == t for t in PLAIN_TILES]))
    def _():
        o_ref[...] = plain()

    @pl.when(j == T_LRU_X)
    def _():
        o_ref[...] = conv()

    @pl.when(j == T_GDN_Q)
    def _():
        o_ref[...] = _l2_heads(_silu(conv()), GDN_DK, GDN_DK ** -0.5)

    @pl.when(j == T_GDN_K)
    def _():
        o_ref[...] = _l2_heads(_silu(conv()), GDN_DK, 1.0)

    @pl.when(j == T_GDN_V)
    def _():
        o_ref[...] = _silu(conv())

    @pl.when((j >= T_MERGE) & (j < T_SMALL))
    def _():
        o_ref[...] = _sigmoid(plain())


def _proj_call(xa, scl, shl, scc, shc, w, b, cw, cb, cos, sin, *, nb, tb, ctx_len, tm):
    r, d = xa.shape
    n = w.shape[1]
    tpb = tb // tm
    kern = functools.partial(_proj_kernel, tm=tm, tpb=tpb, ctx_len=ctx_len, tb=tb)
    return pl.pallas_call(
        kern,
        out_shape=jax.ShapeDtypeStruct((r, n), F32),
        grid=(r // tm, n // TN),
        in_specs=_halo_specs(tm, d, r) + [
            pl.BlockSpec((1, 1, d), lambda i, j: (i // tpb, 0, 0)),
            pl.BlockSpec((1, 1, d), lambda i, j: (i // tpb, 0, 0)),
            pl.BlockSpec((1, d), lambda i, j: (0, 0)),
            pl.BlockSpec((1, d), lambda i, j: (0, 0)),
            pl.BlockSpec((d, TN), lambda i, j: (0, j)),
            pl.BlockSpec((1, TN), lambda i, j: (0, j)),
            pl.BlockSpec((LRU_CONV, TN), lambda i, j: (0, j)),
            pl.BlockSpec((1, TN), lambda i, j: (0, j)),
            pl.BlockSpec((tm, 128), lambda i, j: (i % tpb, 0)),
            pl.BlockSpec((tm, 128), lambda i, j: (i % tpb, 0)),
        ],
        out_specs=pl.BlockSpec((tm, TN), lambda i, j: (i, j)),
        scratch_shapes=[pltpu.VMEM((tm + 2 * HALO, d), BF16), pltpu.VMEM((tm + 2 * HALO, TN), F32)],
        compiler_params=_cparams(("arbitrary", "arbitrary")),
        name="proj",
    )(xa, xa, xa, scl, shl, scc, shc, w, b, cw, cb, cos, sin)


def _bwd_index(i, n_ctx, n_tot):
    return jnp.where(i < n_ctx, n_ctx - 1 - i, n_tot + n_ctx - 1 - i)


def _gla_kernel(qkf_ref, vf_ref, gf_ref, qkb_ref, vb_ref, gb_ref, wg_ref, bg_ref, of_ref, ob_ref, st_ref):
    c = CHUNK
    hk = GLA_HEADS * GLA_DK

    @pl.when(pl.program_id(1) == 0)
    def _():
        st_ref[...] = jnp.zeros_like(st_ref)

    row = lax.broadcasted_iota(jnp.int32, (c, c), 0)
    col = lax.broadcasted_iota(jnp.int32, (c, c), 1)
    for d, (qk_ref, v_ref, g_ref, o_ref) in enumerate(((qkf_ref, vf_ref, gf_ref, of_ref),
                                                       (qkb_ref, vb_ref, gb_ref, ob_ref))):
        tri = (row >= col) if d == 0 else (row <= col)
        tri_bf = jnp.where(tri, 1.0, 0.0).astype(BF16)
        gin = g_ref[:, d * GLA_GATE_RANK:(d + 1) * GLA_GATE_RANK]
        z = _mm3(gin, wg_ref[d]) + bg_ref[d]
        la = (jnp.minimum(z, 0.0) - jnp.log1p(jnp.exp(-jnp.abs(z)))) * (1.0 / GLA_GATE_NORM)
        cum = _cumdot(tri_bf, la)
        last = cum[c - 1:c, :] if d == 0 else cum[0:1, :]
        mid = cum[c // 2 - 1:c // 2, :] if d == 0 else cum[c // 2:c // 2 + 1, :]
        q = qk_ref[:, :hk]
        k = qk_ref[:, hk:]
        qd = (q * jnp.exp(cum - mid)).astype(BF16)
        km = (k * jnp.exp(mid - cum)).astype(BF16)
        kdec = (k * jnp.exp(last - cum)).astype(BF16)
        qs = (q * jnp.exp(cum)).astype(BF16)
        dec = jnp.exp(last)
        vv = v_ref[...].astype(BF16)
        outs = []
        for h in range(GLA_HEADS):
            sl = slice(h * GLA_DK, (h + 1) * GLA_DK)
            vs = slice(h * GLA_DV, (h + 1) * GLA_DV)
            sc = jnp.where(tri, _mm_nt(qd[:, sl], km[:, sl]), 0.0)
            st = st_ref[d, h]
            outs.append(_mm(sc.astype(BF16), vv[:, vs]) + _mm_nt(qs[:, sl], st.astype(BF16)))
            st_ref[d, h] = st * dec[:, sl] + _mm_tn(vv[:, vs], kdec[:, sl])
        o_ref[...] = jnp.concatenate(outs, axis=1)


def _gla_call(p, wg, bg, *, nb, tb, ctx_len):
    r = p.shape[0]
    nct, ncc = tb // CHUNK, ctx_len // CHUNK
    sub = TN // 128
    fwd = lambda b, i: b * nct + i
    bwd = lambda b, i: b * nct + _bwd_index(i, ncc, nct)
    width = GLA_HEADS * GLA_DV
    hk = GLA_HEADS * GLA_DK

    def specs(rowf):
        return [pl.BlockSpec((CHUNK, TN), lambda b, i: (rowf(b, i), T_GLA_QK)),
                pl.BlockSpec((CHUNK, TN), lambda b, i: (rowf(b, i), T_GLA_V)),
                pl.BlockSpec((CHUNK, 128), lambda b, i: (rowf(b, i), T_SMALL * sub))]
    return pl.pallas_call(
        _gla_kernel,
        out_shape=(jax.ShapeDtypeStruct((r, width), F32), jax.ShapeDtypeStruct((r, width), F32)),
        grid=(nb, nct),
        in_specs=specs(fwd) + specs(bwd) + [
            pl.BlockSpec((2, GLA_GATE_RANK, hk), lambda b, i: (0, 0, 0)),
            pl.BlockSpec((2, 1, hk), lambda b, i: (0, 0, 0))],
        out_specs=(pl.BlockSpec((CHUNK, width), lambda b, i: (fwd(b, i), 0)),
                   pl.BlockSpec((CHUNK, width), lambda b, i: (bwd(b, i), 0))),
        scratch_shapes=[pltpu.VMEM((2, GLA_HEADS, GLA_DV, GLA_DK), F32)],
        compiler_params=_cparams(("arbitrary", "arbitrary")),
        name="gla",
    )(p, p, p, p, p, p, wg, bg)


def _lru_kernel(x_ref, w_ref, gb_ref, lam_ref, o_ref, a_s, b_s, h_s, *, tl):
    d = pl.program_id(1)

    @pl.when(pl.program_id(2) == 0)
    def _():
        h_s[...] = jnp.zeros_like(h_s)

    x = x_ref[...]
    gt = _mm(x.astype(BF16), w_ref[0]) + gb_ref[0]
    r = _sigmoid(gt[:, :LRU_WIDTH])
    ig = _sigmoid(gt[:, LRU_WIDTH:])
    log_a = (-LRU_C) * r * _softplus(-lam_ref[0])
    a_s[...] = jnp.exp(log_a)
    b_s[...] = jnp.sqrt(1.0 - jnp.exp(2.0 * log_a)) * ig * x

    def body(s, h):
        rr = jnp.where(d == 0, s, tl - 1 - s)
        h = a_s[pl.ds(rr, 1), :] * h + b_s[pl.ds(rr, 1), :]
        o_ref[0, pl.ds(rr, 1), :] = h
        return h
    h_s[...] = lax.fori_loop(0, tl, body, h_s[...], unroll=8)


def _lru_call(p, wbd, gb, lam, *, nb, tb, ctx_len, tl):
    r = p.shape[0]
    ntl, nctl = tb // tl, ctx_len // tl

    def rowblk(b, d, i):
        return b * ntl + jnp.where(d == 0, i, _bwd_index(i, nctl, ntl))
    return pl.pallas_call(
        functools.partial(_lru_kernel, tl=tl),
        out_shape=jax.ShapeDtypeStruct((2, r, LRU_WIDTH), F32),
        grid=(nb, 2, ntl),
        in_specs=[pl.BlockSpec((tl, TN), lambda b, d, i: (rowblk(b, d, i), T_LRU_X)),
                  pl.BlockSpec((1, LRU_WIDTH, 2 * LRU_WIDTH), lambda b, d, i: (d, 0, 0)),
                  pl.BlockSpec((1, 1, 2 * LRU_WIDTH), lambda b, d, i: (d, 0, 0)),
                  pl.BlockSpec((1, 1, LRU_WIDTH), lambda b, d, i: (d, 0, 0))],
        out_specs=pl.BlockSpec((1, tl, LRU_WIDTH), lambda b, d, i: (d, rowblk(b, d, i), 0)),
        scratch_shapes=[pltpu.VMEM((tl, LRU_WIDTH), F32), pltpu.VMEM((tl, LRU_WIDTH), F32),
                        pltpu.VMEM((1, LRU_WIDTH), F32)],
        compiler_params=_cparams(("arbitrary", "arbitrary", "arbitrary")),
        name="lru",
    )(p, wbd, gb, lam)


def _unit_tri_inverse(a, row, col):
    eye = jnp.where(row == col, 1.0, 0.0)
    blk = lambda s: (row // s) == (col // s)
    a0 = jnp.where(blk(8), a, 0.0)
    p = _mm3(a0, a0)
    t = (eye - a0) + _mm3(eye - a0, p)
    p = _mm3(p, p)
    t = t + _mm3(t, p)
    s = 8
    while s < CHUNK:
        off = jnp.where(blk(2 * s) & jnp.logical_not(blk(s)), a, 0.0)
        t = t - _mm3(t, _mm3(off, t))
        s *= 2
    return t


def _gdn_kernel(qkvf_ref, gf_ref, qkvb_ref, gb_ref, alog_ref, dtb_ref, of_ref, ob_ref, s_ref):
    c = CHUNK
    hk = GDN_HEADS * GDN_DK

    @pl.when(pl.program_id(1) == 0)
    def _():
        s_ref[...] = jnp.zeros_like(s_ref)

    row = lax.broadcasted_iota(jnp.int32, (c, c), 0)
    col = lax.broadcasted_iota(jnp.int32, (c, c), 1)
    neg_a = -jnp.exp(alog_ref[...])
    for d, (qkv_ref, g_ref, o_ref) in enumerate(((qkvf_ref, gf_ref, of_ref), (qkvb_ref, gb_ref, ob_ref))):
        incl = (row >= col) if d == 0 else (row <= col)
        strict = (row > col) if d == 0 else (row < col)
        tri_bf = jnp.where(incl, 1.0, 0.0).astype(BF16)
        gblk = g_ref[...]
        g_all = neg_a * _softplus(gblk + dtb_ref[...])
        beta_all = _sigmoid(gblk)
        gc_all = _cumdot(tri_bf, g_all)
        gc_t = gc_all.T
        outs = []
        for h in range(GDN_HEADS):
            la = SM_A + d * GDN_HEADS + h
            lb = SM_B + d * GDN_HEADS + h
            gcol = gc_all[:, la:la + 1]
            grow = gc_t[la:la + 1, :]
            beta = beta_all[:, lb:lb + 1]
            glast = gcol[c - 1:c, :] if d == 0 else gcol[0:1, :]
            decay = jnp.exp(jnp.where(incl, gcol - grow, -jnp.inf))
            qh = qkv_ref[:, h * GDN_DK:(h + 1) * GDN_DK]
            kh = qkv_ref[:, hk + h * GDN_DK:hk + (h + 1) * GDN_DK]
            vh = qkv_ref[:, 2 * hk + h * GDN_DV:2 * hk + (h + 1) * GDN_DV]
            kb = kh * beta
            vb = vh * beta
            k_bf = kh.astype(BF16)
            a = jnp.where(strict, _mm_nt(kb.astype(BF16), k_bf) * decay, 0.0)
            attn = jnp.where(incl, _mm_nt(qh.astype(BF16), k_bf) * decay, 0.0)
            t = _unit_tri_inverse(a, row, col)
            egc = jnp.exp(gcol)
            sol = _mm3(t, jnp.concatenate([vb, kb * egc], axis=1))
            u = sol[:, :GDN_DV]
            w = sol[:, GDN_DV:]
            kdec = kh * jnp.exp(glast - gcol)
            qdec = qh * egc
            s = s_ref[d, h]
            s_bf = s.astype(BF16)
            v_new = u - _mm(w.astype(BF16), s_bf)
            vn_bf = v_new.astype(BF16)
            outs.append(_mm(qdec.astype(BF16), s_bf) + _mm(attn.astype(BF16), vn_bf))
            s_ref[d, h] = s * jnp.exp(glast) + _mm_tn(kdec.astype(BF16), vn_bf)
        o_ref[...] = jnp.concatenate(outs, axis=1)


def _gdn_call(p, alog, dtb, *, nb, tb, ctx_len):
    r = p.shape[0]
    nct, ncc = tb // CHUNK, ctx_len // CHUNK
    sub = TN // 128
    fwd = lambda b, i: b * nct + i
    bwd = lambda b, i: b * nct + _bwd_index(i, ncc, nct)
    width = GDN_HEADS * GDN_DV
    qkv_w = 3 * TN

    def specs(rowf):
        return [pl.BlockSpec((CHUNK, qkv_w), lambda b, i: (rowf(b, i), T_GDN_Q * TN // qkv_w)),
                pl.BlockSpec((CHUNK, 128), lambda b, i: (rowf(b, i), T_SMALL * sub))]
    return pl.pallas_call(
        _gdn_kernel,
        out_shape=(jax.ShapeDtypeStruct((r, width), F32), jax.ShapeDtypeStruct((r, width), F32)),
        grid=(nb, nct),
        in_specs=specs(fwd) + specs(bwd) + [pl.BlockSpec((1, 128), lambda b, i: (0, 0)),
                                            pl.BlockSpec((1, 128), lambda b, i: (0, 0))],
        out_specs=(pl.BlockSpec((CHUNK, width), lambda b, i: (fwd(b, i), 0)),
                   pl.BlockSpec((CHUNK, width), lambda b, i: (bwd(b, i), 0))),
        scratch_shapes=[pltpu.VMEM((2, GDN_HEADS, GDN_DK, GDN_DV), F32)],
        compiler_params=_cparams(("arbitrary", "arbitrary")),
        name="gdn",
    )(p, p, p, p, alog, dtb)


def _na_kernel(q_ref, k0_ref, k1_ref, k2_ref, v0_ref, v1_ref, v2_ref, kc_ref, vc_ref, bias_ref, o_ref,
               kbuf, vbuf, *, rows):
    jj = pl.program_id(1)
    tq = NA_ROWS * GRID_W
    hd = NA_HD

    @pl.when(jj == 0)
    def _():
        outs = []
        for h in range(NA_HEADS):
            sl = slice(h * hd, (h + 1) * hd)
            s = _mm_nt(q_ref[:, sl].astype(BF16), k1_ref[:, sl].astype(BF16))
            m = jnp.max(s, axis=-1, keepdims=True)
            e = jnp.exp(s - m)
            l = jnp.sum(e, axis=-1, keepdims=True)
            outs.append(_mm(e.astype(BF16), v1_ref[:, sl].astype(BF16)) / l)
        o_ref[...] = jnp.concatenate(outs, axis=1)

    @pl.when(jj > 0)
    def _():
        jp = jj - 1
        for n, (kr, vr) in enumerate(((k0_ref, v0_ref), (k1_ref, v1_ref), (k2_ref, v2_ref))):
            kbuf[n * tq:(n + 1) * tq, :] = kr[...].astype(BF16)
            vbuf[n * tq:(n + 1) * tq, :] = vr[...].astype(BF16)
        kc = kc_ref[...].astype(BF16)
        vc = vc_ref[...].astype(BF16)
        for a in range(NA_ROWS):
            r = NA_ROWS * jp + a
            rs = jnp.clip(r - NA_WIN_R // 2, 0, rows - NA_WIN_R)
            start = pl.multiple_of((rs - NA_ROWS * (jp - 1)) * GRID_W, GRID_W)
            e_idx = r - rs
            outs = []
            for h in range(NA_HEADS):
                sl = slice(h * hd, (h + 1) * hd)
                qh = q_ref[a * GRID_W:(a + 1) * GRID_W, sl].astype(BF16)
                kw = kbuf[pl.ds(start, NA_WIN_R * GRID_W), sl]
                vw = vbuf[pl.ds(start, NA_WIN_R * GRID_W), sl]
                s_loc = _mm_nt(qh, kw) + bias_ref[e_idx, h]
                s_ctx = _mm_nt(qh, kc[:, sl])
                m = jnp.maximum(jnp.max(s_loc, axis=-1, keepdims=True), jnp.max(s_ctx, axis=-1, keepdims=True))
                p_loc = jnp.exp(s_loc - m)
                p_ctx = jnp.exp(s_ctx - m)
                l = jnp.sum(p_loc, axis=-1, keepdims=True) + jnp.sum(p_ctx, axis=-1, keepdims=True)
                o = _mm(p_loc.astype(BF16), vw) + _mm(p_ctx.astype(BF16), vc[:, sl])
                outs.append(o / l)
            o_ref[a * GRID_W:(a + 1) * GRID_W, :] = jnp.concatenate(outs, axis=1)


def _na_call(p, bias, *, nb, tb, ctx_len):
    r = p.shape[0]
    tq = NA_ROWS * GRID_W
    assert ctx_len == tq, "context block must be exactly one query block"
    nblk = tb // tq
    nlb = nblk - 1
    rows = (tb - ctx_len) // GRID_W
    assert rows >= NA_WIN_R and rows % NA_ROWS == 0
    width = NA_HEADS * NA_HD
    cur = lambda b, j: b * nblk + j
    prv = lambda b, j: b * nblk + 1 + jnp.clip(j - 2, 0, nlb - 1)
    nxt = lambda b, j: b * nblk + 1 + jnp.clip(j, 0, nlb - 1)
    blk = lambda rowf, t: pl.BlockSpec((tq, TN), lambda b, j: (rowf(b, j), t))
    return pl.pallas_call(
        functools.partial(_na_kernel, rows=rows),
        out_shape=jax.ShapeDtypeStruct((r, width), F32),
        grid=(nb, nblk),
        in_specs=[blk(cur, T_NA_Q),
                  blk(prv, T_NA_K), blk(cur, T_NA_K), blk(nxt, T_NA_K),
                  blk(prv, T_NA_V), blk(cur, T_NA_V), blk(nxt, T_NA_V),
                  pl.BlockSpec((ctx_len, TN), lambda b, j: (b * nblk, T_NA_K)),
                  pl.BlockSpec((ctx_len, TN), lambda b, j: (b * nblk, T_NA_V)),
                  pl.BlockSpec(bias.shape, lambda b, j: (0, 0, 0, 0), pipeline_mode=pl.Buffered(1))],
        out_specs=pl.BlockSpec((tq, width), lambda b, j: (cur(b, j), 0)),
        scratch_shapes=[pltpu.VMEM((3 * tq, TN), BF16), pltpu.VMEM((3 * tq, TN), BF16)],
        compiler_params=_cparams(("arbitrary", "arbitrary")),
        name="na",
    )(p, p, p, p, p, p, p, p, p, bias)


def _na_bias_table(rpb):
    c = np.arange(GRID_W)
    cs = np.clip(c - NA_WIN_C // 2, 0, GRID_W - NA_WIN_C)
    kc = np.arange(GRID_W)
    inwin = (kc[None, :] >= cs[:, None]) & (kc[None, :] < cs[:, None] + NA_WIN_C)
    dc = np.clip(kc[None, :] - c[:, None] + NA_WIN_C - 1, 0, 2 * NA_WIN_C - 2)
    e = np.arange(NA_WIN_R)
    i = np.arange(NA_WIN_R)
    dr = i[None, :] - e[:, None] + NA_WIN_R - 1
    tab = rpb[:, dr[:, None, :, None], dc[None, :, None, :]]
    tab = jnp.where(inwin[None, None, :, None, :], tab, NEG_BIG)
    tab = jnp.transpose(tab, (1, 0, 2, 3, 4))
    return tab.reshape(NA_WIN_R, NA_HEADS, GRID_W, NA_WIN_R * GRID_W)


def _head_rms(o, g, width):
    outs = []
    for h in range(o.shape[1] // width):
        xs = o[:, h * width:(h + 1) * width]
        ms = jnp.mean(xs * xs, axis=-1, keepdims=True)
        outs.append(xs * lax.rsqrt(ms + NORM_EPS) * g)
    return jnp.concatenate(outs, axis=1)


def _merge_kernel(gof_ref, gob_ref, gg_ref, lhf_ref, lhb_ref, ly_ref, dof_ref, dob_ref, dz_ref, na_ref,
                  m0_ref, m1_ref, m2_ref, m3_ref, x_ref, g1l_ref, g1c_ref, wbr_ref, wout_ref, bout_ref,
                  gng_ref, gnd_ref, lng_ref, lnb_ref, o_ref, *, tm, tpb, ctx_len, alpha):
    p0 = (pl.program_id(0) % tpb) * tm
    y_a = _head_rms(gof_ref[...] + gob_ref[...], gng_ref[...], GLA_DV) * _silu(gg_ref[...])
    y_b = (lhf_ref[0] + lhb_ref[0]) * _gelu_tanh(ly_ref[...])
    y_c = _head_rms(dof_ref[...] + dob_ref[...], gnd_ref[...], GDN_DV) * _silu(dz_ref[...])
    y_d = na_ref[...]
    m = None
    for n, (y, g_ref) in enumerate(((y_a, m0_ref), (y_b, m1_ref), (y_c, m2_ref), (y_d, m3_ref))):
        term = g_ref[...] * _mm(y.astype(BF16), wbr_ref[n])
        m = term if m is None else m + term
    out = _mm(m.astype(BF16), wout_ref[...]) + bout_ref[...]
    is_ctx = _row_pos(tm, p0) < ctx_len
    g1 = jnp.where(is_ctx, g1c_ref[...], g1l_ref[0])
    o_ref[...] = _layer_norm(alpha * x_ref[...] + g1 * out, lng_ref[...], lnb_ref[...])


def _merge_call(p, gof, gob, lh, dof, dob, na, xa, g1l, g1c, wbr, wout, bout, gng, gnd, lng, lnb,
                *, tb, ctx_len, tm, alpha):
    r, d = xa.shape
    tpb = tb // tm
    bw = BRANCH_W
    row = lambda w: pl.BlockSpec((tm, w), lambda i: (i, 0))
    ptile = lambda t, w: pl.BlockSpec((tm, w), lambda i: (i, t * TN // w))
    const = lambda a: pl.BlockSpec(a.shape, lambda i: (0,) * a.ndim)
    kern = functools.partial(_merge_kernel, tm=tm, tpb=tpb, ctx_len=ctx_len, alpha=alpha)
    return pl.pallas_call(
        kern,
        out_shape=jax.ShapeDtypeStruct((r, d), F32),
        grid=(r // tm,),
        in_specs=[row(bw), row(bw), ptile(T_GLA_G, bw),
                  pl.BlockSpec((1, tm, bw), lambda i: (0, i, 0)), pl.BlockSpec((1, tm, bw), lambda i: (1, i, 0)),
                  ptile(T_LRU_Y, bw),
                  row(bw), row(bw), ptile(T_GDN_Z, bw),
                  row(bw),
                  pl.BlockSpec((tm, d), lambda i: (i, T_MERGE * TN // d + 0)),
                  pl.BlockSpec((tm, d), lambda i: (i, T_MERGE * TN // d + 1)),
                  pl.BlockSpec((tm, d), lambda i: (i, T_MERGE * TN // d + 2)),
                  pl.BlockSpec((tm, d), lambda i: (i, T_MERGE * TN // d + 3)),
                  row(d),
                  pl.BlockSpec((1, 1, d), lambda i: (i // tpb, 0, 0)),
                  const(g1c), const(wbr), const(wout), const(bout), const(gng), const(gnd), const(lng), const(lnb)],
        out_specs=row(d),
        compiler_params=_cparams(("arbitrary",)),
        name="merge",
    )(gof, gob, p, lh, lh, p, dof, dob, p, na, p, p, p, p, xa, g1l, g1c, wbr, wout, bout, gng, gnd, lng, lnb)


def _ffn_kernel(x_ref, xp_ref, xn_ref, scl_ref, shl_ref, gl_ref, scc_ref, shc_ref, gc_ref, wa_ref, wb_ref,
                ba_ref, bb_ref, cw_ref, cb_ref, wd_ref, bd_ref, lng_ref, lnb_ref, o_ref,
                lhs_ref, a_ref, acc_ref, *, tm, tpb, ctx_len, tb, nc, alpha):
    i = pl.program_id(0)
    c = pl.program_id(1)
    p0 = (i % tpb) * tm

    @pl.when(c == 0)
    def _():
        _build_lhs(lhs_ref, x_ref, xp_ref, xn_ref, scl_ref[0], shl_ref[0], scc_ref[...], shc_ref[...],
                   p0, tm, ctx_len)
        acc_ref[...] = jnp.zeros_like(acc_ref)

    a_ref[...] = _mm(lhs_ref[...], wa_ref[...]) + ba_ref[...]
    a = _dwconv(a_ref, cw_ref, p0, tm, ctx_len, tb, FFN_CONV // 2) + cb_ref[...]
    gate = _mm(lhs_ref[HALO:HALO + tm, :], wb_ref[...]) + bb_ref[...]
    acc_ref[...] += _mm((_silu(a) * gate).astype(BF16), wd_ref[...])

    @pl.when(c == nc - 1)
    def _():
        f = acc_ref[...] + bd_ref[...]
        is_ctx = _row_pos(tm, p0) < ctx_len
        g2 = jnp.where(is_ctx, gc_ref[...], gl_ref[0])
        o_ref[...] = _layer_norm(alpha * x_ref[...] + g2 * f, lng_ref[...], lnb_ref[...])


def _ffn_call(xa, scl, shl, gl, scc, shc, gc, w_up, b_up, cw, cb, w_down, b_down, lng, lnb,
              *, tb, ctx_len, tm, tc, alpha):
    r, d = xa.shape
    dff = w_down.shape[0]
    nc = dff // tc
    tpb = tb // tm
    lat = pl.BlockSpec((1, 1, d), lambda i, c: (i // tpb, 0, 0))
    vec = pl.BlockSpec((1, d), lambda i, c: (0, 0))
    kern = functools.partial(_ffn_kernel, tm=tm, tpb=tpb, ctx_len=ctx_len, tb=tb, nc=nc, alpha=alpha)
    return pl.pallas_call(
        kern,
        out_shape=jax.ShapeDtypeStruct((r, d), F32),
        grid=(r // tm, nc),
        in_specs=_halo_specs(tm, d, r) + [
            lat, lat, lat, vec, vec, vec,
            pl.BlockSpec((d, tc), lambda i, c: (0, c)),
            pl.BlockSpec((d, tc), lambda i, c: (0, nc + c)),
            pl.BlockSpec((1, tc), lambda i, c: (0, c)),
            pl.BlockSpec((1, tc), lambda i, c: (0, nc + c)),
            pl.BlockSpec((FFN_CONV, tc), lambda i, c: (0, c)),
            pl.BlockSpec((1, tc), lambda i, c: (0, c)),
            pl.BlockSpec((tc, d), lambda i, c: (c, 0)),
            vec, vec, vec],
        out_specs=pl.BlockSpec((tm, d), lambda i, c: (i, 0)),
        scratch_shapes=[pltpu.VMEM((tm + 2 * HALO, d), BF16), pltpu.VMEM((tm + 2 * HALO, tc), F32),
                        pltpu.VMEM((tm, d), F32)],
        compiler_params=_cparams(("arbitrary", "arbitrary")),
        name="ffn",
    )(xa, xa, xa, scl, shl, gl, scc, shc, gc, w_up, w_up, b_up, b_up, cw, cb, w_down, b_down, lng, lnb)


def _pad_cols(a, width):
    return jnp.pad(a, ((0, 0), (0, width - a.shape[1])))


def _in_proj_params(w_in, b_in, lru_conv_w, lru_conv_b, gdn_conv_w):
    sizes = (GLA_HEADS * GLA_DK, GLA_HEADS * GLA_DK, GLA_HEADS * GLA_DV, GLA_HEADS * GLA_DV,
             GLA_GATE_RANK, GLA_GATE_RANK, LRU_WIDTH, LRU_WIDTH,
             GDN_HEADS * GDN_DK, GDN_HEADS * GDN_DK, GDN_HEADS * GDN_DV, GDN_HEADS * GDN_DV,
             2 * GDN_HEADS, 2 * GDN_HEADS, NA_HEADS * NA_HD, NA_HEADS * NA_HD, NA_HEADS * NA_HD)
    names = ('gla_q', 'gla_k', 'gla_v', 'gla_g', 'gla_fw', 'gla_bw', 'lru_x', 'lru_y', 'gdn_q', 'gdn_k',
             'gdn_v', 'gdn_z', 'gdn_a', 'gdn_b', 'na_q', 'na_k', 'na_v')
    offs = np.concatenate([[0], np.cumsum(sizes)])
    wb = jnp.concatenate([w_in, b_in[None, :]], axis=0)
    col = {n: wb[:, offs[k]:offs[k + 1]] for k, n in enumerate(names)}
    col['merge'] = wb[:, offs[-1]:]
    small = _pad_cols(jnp.concatenate([col['gla_fw'], col['gla_bw'], col['gdn_a'], col['gdn_b']], axis=1), TN)
    tiles = [jnp.concatenate([col['gla_q'] * (GLA_DK ** -0.5), col['gla_k']], axis=1),
             col['gla_v'], col['gla_g'],
             col['gdn_q'], col['gdn_k'], col['gdn_v'], col['gdn_z'], col['lru_x'], col['lru_y'],
             col['na_q'] * (NA_HD ** -0.5), col['na_k'], col['na_v'], col['merge'], small]
    wb = jnp.concatenate(tiles, axis=1)
    n = wb.shape[1]
    assert n == N_TILES * TN
    cw = jnp.zeros((LRU_CONV, n), F32)
    cw = cw.at[:, T_LRU_X * TN:(T_LRU_X + 1) * TN].set(lru_conv_w)
    cw = cw.at[:, T_GDN_Q * TN:(T_GDN_V + 1) * TN].set(gdn_conv_w)
    cb = jnp.zeros((1, n), F32).at[0, T_LRU_X * TN:(T_LRU_X + 1) * TN].set(lru_conv_b)
    return wb[:-1].astype(BF16), wb[-1:], cw, cb


def _rope_tables(ctx_len, seq_len):
    quarter = GLA_DK // 4
    t = jnp.arange(seq_len)
    inv = ROPE_BASE ** (-jnp.arange(quarter, dtype=F32) / quarter)
    ang = jnp.concatenate([(t // GRID_W).astype(F32)[:, None] * inv,
                           (t % GRID_W).astype(F32)[:, None] * inv], -1)
    cos = jnp.concatenate([jnp.ones((ctx_len, 2 * quarter), F32), jnp.cos(ang)], axis=0)
    sin = jnp.concatenate([jnp.zeros((ctx_len, 2 * quarter), F32), jnp.sin(ang)], axis=0)
    return jnp.concatenate([cos, cos, cos, cos], axis=1), jnp.concatenate([-sin, sin, -sin, sin], axis=1)


def _lru_gate_matrix(gate_w):
    eye = jnp.eye(LRU_BLOCKS, dtype=F32)
    dense = jnp.einsum('xgnde,nm->xndgme', gate_w, eye)
    return dense.reshape(2, LRU_WIDTH, 2 * LRU_WIDTH).astype(BF16)


def _small_lane_vec(v):
    return jnp.zeros((1, 128), F32).at[0, SM_A:SM_A + 2 * GDN_HEADS].set(v.reshape(-1))


def _row_tile(tb, target):
    best = HALO
    for t in range(HALO, target + 1, HALO):
        if tb % t == 0:
            best = t
    return best


def kernel(x, c, ctx, c_ctx, w_mod, b_mod, w_in, b_in, gla_w_gate, gla_b_gate, gla_norm, lru_conv_w, lru_conv_b,
           lru_gate_w, lru_gate_b, lru_lambda, gdn_conv_w, gdn_a_log, gdn_dt_bias, gdn_norm, na_rpb, w_branch,
           w_out, b_out, ln1_g, ln1_b, ffn_w_up, ffn_b_up, ffn_conv_w, ffn_conv_b, ffn_w_down, ffn_b_down,
           ln2_g, ln2_b, *, row_tile=1280):
    nb, seq_len, d = x.shape
    ctx_len = ctx.shape[1]
    depth = w_mod.shape[0]
    tb = ctx_len + seq_len
    alpha = (2 * depth) ** 0.25
    tm = _row_tile(tb, row_tile)
    tl = NA_ROWS * GRID_W
    dims = dict(nb=nb, tb=tb, ctx_len=ctx_len)

    xa = jnp.concatenate([ctx, x], axis=1).reshape(nb * tb, d)
    crows = jnp.zeros((8, d), F32).at[:nb].set(c).at[nb].set(c_ctx)
    mods = _mod_call(crows, w_mod, b_mod)
    cos, sin = _rope_tables(ctx_len, seq_len)

    for l in range(depth):
        lat = [mods[l, :nb, k * d:(k + 1) * d].reshape(nb, 1, d) for k in range(6)]
        cx = [mods[l, nb:nb + 1, k * d:(k + 1) * d] for k in range(6)]
        w, b, cw, cb = _in_proj_params(w_in[l], b_in[l], lru_conv_w[l], lru_conv_b[l], gdn_conv_w[l])
        p = _proj_call(xa, lat[1], lat[0], cx[1], cx[0], w, b, cw, cb, cos, sin, tm=tm, **dims)

        gof, gob = _gla_call(p, gla_w_gate[l], gla_b_gate[l].reshape(2, 1, -1), **dims)
        lh = _lru_call(p, _lru_gate_matrix(lru_gate_w[l]), lru_gate_b[l].reshape(2, 1, -1),
                       lru_lambda[l].reshape(2, 1, -1), tl=tl, **dims)
        dof, dob = _gdn_call(p, _small_lane_vec(gdn_a_log[l]), _small_lane_vec(gdn_dt_bias[l]), **dims)
        na = _na_call(p, _na_bias_table(na_rpb[l]), **dims)

        xa = _merge_call(p, gof, gob, lh, dof, dob, na, xa, lat[2], cx[2],
                         w_branch[l].astype(BF16), w_out[l].astype(BF16), b_out[l].reshape(1, d),
                         gla_norm[l].reshape(1, -1),
                         gdn_norm[l].reshape(1, -1), ln1_g[l].reshape(1, d), ln1_b[l].reshape(1, d),
                         tb=tb, ctx_len=ctx_len, tm=tl, alpha=alpha)
        xa = _ffn_call(xa, lat[4], lat[3], lat[5], cx[4], cx[3], cx[5],
                       ffn_w_up[l].astype(BF16), ffn_b_up[l].reshape(1, -1), ffn_conv_w[l],
                       ffn_conv_b[l].reshape(1, -1), ffn_w_down[l].astype(BF16), ffn_b_down[l].reshape(1, d),
                       ln2_g[l].reshape(1, d), ln2_b[l].reshape(1, d),
                       tb=tb, ctx_len=ctx_len, tm=tm, tc=256, alpha=alpha)
    return xa.reshape(nb, tb, d)[:, ctx_len:, :]
```

```python
import functools
import math

import jax
import jax.numpy as jnp
import numpy as np
from jax import lax
from jax.experimental import pallas as pl
from jax.experimental.pallas import tpu as pltpu

F32 = jnp.float32
BF16 = jnp.bfloat16

GRID_W = 64
CHUNK = 64
GLA_HEADS, GLA_DK, GLA_DV = 4, 64, 128
GLA_GATE_RANK = 16
GLA_GATE_NORM = 16.0
ROPE_BASE = 10000.0
LRU_WIDTH, LRU_BLOCKS, LRU_CONV, LRU_C = 512, 8, 4, 8.0
GDN_HEADS, GDN_DK, GDN_DV, GDN_CONV = 4, 128, 128, 4
NA_HEADS, NA_HD, NA_WIN_R, NA_WIN_C = 8, 64, 8, 16
FFN_CONV = 3
LN_EPS = 1e-5
NORM_EPS = 1e-6
BRANCH_W = 512

HALO = 16
ROW_TILE = 1280
TN = 512
NA_ROWS = 4
VMEM_LIMIT = 56 * 1024 * 1024
NEG_BIG = -1e30

T_GLA_QK, T_GLA_V, T_GLA_G = 0, 1, 2
T_GDN_Q, T_GDN_K, T_GDN_V, T_GDN_Z = 3, 4, 5, 6
T_LRU_X, T_LRU_Y = 7, 8
T_NA_Q, T_NA_K, T_NA_V = 9, 10, 11
T_MERGE = 12
T_SMALL = 20
N_TILES = 21
PLAIN_TILES = (T_GLA_V, T_GLA_G, T_GDN_Z, T_LRU_Y, T_NA_Q, T_NA_K, T_NA_V)
assert T_SMALL == N_TILES - 1
SM_FW, SM_BW, SM_A, SM_B = 0, 16, 32, 40


def _cparams(sem):
    return pltpu.CompilerParams(dimension_semantics=sem, vmem_limit_bytes=VMEM_LIMIT)


def _mm(a, b):
    return jnp.dot(a, b, preferred_element_type=F32)


def _mm_nt(a, b):
    return lax.dot_general(a, b, (((1,), (1,)), ((), ())), preferred_element_type=F32)


def _mm_tn(a, b):
    return lax.dot_general(a, b, (((0,), (0,)), ((), ())), preferred_element_type=F32)


def _split2(a):
    hi = a.astype(BF16)
    lo = (a - hi.astype(F32)).astype(BF16)
    return hi, lo


def _mm3(a, b):
    ah, al = _split2(a)
    bh, bl = _split2(b)
    return _mm(ah, bh) + (_mm(ah, bl) + _mm(al, bh))


def _cumdot(tri_bf, x):
    h1 = x.astype(BF16)
    r1 = x - h1.astype(F32)
    h2 = r1.astype(BF16)
    h3 = (r1 - h2.astype(F32)).astype(BF16)
    return _mm(tri_bf, h1) + (_mm(tri_bf, h2) + _mm(tri_bf, h3))


def _sigmoid(x):
    return 1.0 / (1.0 + jnp.exp(-x))


def _silu(x):
    return x * _sigmoid(x)


def _softplus(x):
    return jnp.maximum(x, 0.0) + jnp.log1p(jnp.exp(-jnp.abs(x)))


def _gelu_tanh(x):
    return x * (0.5 * (1.0 + jnp.tanh(math.sqrt(2.0 / math.pi) * (x + 0.044715 * (x * x * x)))))


def _layer_norm(y, g, b):
    mu = jnp.mean(y, axis=-1, keepdims=True)
    yc = y - mu
    var = jnp.mean(yc * yc, axis=-1, keepdims=True)
    return yc * lax.rsqrt(var + LN_EPS) * g + b


def _row_pos(n, off):
    return lax.broadcasted_iota(jnp.int32, (n, 1), 0) + off


def _mod_kernel(c_ref, w_ref, b_ref, o_ref):
    cv = c_ref[...]
    o_ref[0] = jnp.dot(_silu(cv), w_ref[0], precision=lax.Precision.HIGHEST,
                       preferred_element_type=F32) + b_ref[0]


def _mod_call(crows, w_mod, b_mod):
    depth, d, n = w_mod.shape
    tn = n // 4
    return pl.pallas_call(
        _mod_kernel,
        out_shape=jax.ShapeDtypeStruct((depth, 8, n), F32),
        grid=(depth, n // tn),
        in_specs=[pl.BlockSpec((8, d), lambda l, j: (0, 0)),
                  pl.BlockSpec((1, d, tn), lambda l, j: (l, 0, j)),
                  pl.BlockSpec((1, 1, tn), lambda l, j: (l, 0, j))],
        out_specs=pl.BlockSpec((1, 8, tn), lambda l, j: (l, 0, j)),
        compiler_params=_cparams(("arbitrary", "arbitrary")),
        name="mod",
    )(crows, w_mod, b_mod.reshape(depth, 1, n))


def _build_lhs(lhs_ref, x_ref, xp_ref, xn_ref, scl, shl, scc, shc, p0, tm, ctx_len):
    def mod(xv, p):
        is_ctx = (p >= 0) & (p < ctx_len)
        sc = jnp.where(is_ctx, scc, scl)
        sh = jnp.where(is_ctx, shc, shl)
        return (xv * (1.0 + sc) + sh).astype(BF16)
    lhs_ref[0:HALO, :] = mod(xp_ref[...], _row_pos(HALO, p0 - HALO))
    lhs_ref[HALO:HALO + tm, :] = mod(x_ref[...], _row_pos(tm, p0))
    lhs_ref[HALO + tm:, :] = mod(xn_ref[...], _row_pos(HALO, p0 + tm))


def _dwconv(acc_ref, cw_ref, left, r0, n, p0=None, ctx_len=None, tb=None):
    if p0 is not None:
        p = _row_pos(n, p0 + r0)
        in_ctx = p < ctx_len
        seg_lo = jnp.where(in_ctx, 0, ctx_len)
        seg_hi = jnp.where(in_ctx, ctx_len, tb)
    out = None
    for k in range(cw_ref.shape[0]):
        d = k - left
        v = acc_ref[pl.ds(HALO + r0 + d, n), :]
        if p0 is not None:
            v = jnp.where((p + d >= seg_lo) & (p + d < seg_hi), v, 0.0)
        term = v * cw_ref[k:k + 1, :]
        out = term if out is None else out + term
    return out


def _segment_end_windows(tm, ctx_len):
    starts = {0, tm - HALO}
    if ctx_len % tm:
        starts |= {ctx_len % tm - HALO, ctx_len % tm}
    return sorted(starts)


def _halo_specs(tm, d, r):
    hb = tm // HALO
    return [pl.BlockSpec((tm, d), lambda i, j: (i, 0)),
            pl.BlockSpec((HALO, d), lambda i, j: (jnp.maximum(i * hb - 1, 0), 0)),
            pl.BlockSpec((HALO, d), lambda i, j: (jnp.minimum((i + 1) * hb, r // HALO - 1), 0))]


def _swap_half_heads(x):
    n = x.shape[1]
    lane = lax.broadcasted_iota(jnp.int32, x.shape, 1)
    return jnp.where((lane % 64) < 32, pltpu.roll(x, n - 32, 1), pltpu.roll(x, 32, 1))


def _l2_heads(x, width, scale):
    outs = []
    for h in range(x.shape[1] // width):
        xs = x[:, h * width:(h + 1) * width]
        ss = jnp.sum(xs * xs, axis=-1, keepdims=True)
        y = xs * lax.rsqrt(ss + NORM_EPS)
        outs.append(y * scale if scale != 1.0 else y)
    return jnp.concatenate(outs, axis=1)


def _proj_kernel(x_ref, xp_ref, xn_ref, scl_ref, shl_ref, scc_ref, shc_ref, w_ref, b_ref, cw_ref, cb_ref,
                 cos_ref, sin_ref, o_ref, os_ref, lhs_ref, acc_ref, *, tm, tpb, ctx_len, tb):
    i = pl.program_id(0)
    j = pl.program_id(1)
    p0 = (i % tpb) * tm

    @pl.when(j == 0)
    def _():
        _build_lhs(lhs_ref, x_ref, xp_ref, xn_ref, scl_ref[0], shl_ref[0], scc_ref[...], shc_ref[...],
                   p0, tm, ctx_len)

    def plain():
        return _mm(lhs_ref[HALO:HALO + tm, :], w_ref[...]) + b_ref[...]

    def conv_then(epilogue):
        acc_ref[...] = _mm(lhs_ref[...], w_ref[...]) + b_ref[...]
        left = LRU_CONV // 2
        o_ref[...] = epilogue(_dwconv(acc_ref, cw_ref, left, 0, tm) + cb_ref[...]).astype(o_ref.dtype)
        for r0 in _segment_end_windows(tm, ctx_len):
            fixed = _dwconv(acc_ref, cw_ref, left, r0, HALO, p0, ctx_len, tb) + cb_ref[...]
            o_ref[r0:r0 + HALO, :] = epilogue(fixed).astype(o_ref.dtype)

    @pl.when(j == T_GLA_QK)
    def _():
        a = plain()
        reps = a.shape[1] // cos_ref.shape[1]
        cos = jnp.concatenate([cos_ref[...]] * reps, axis=1)
        sin = jnp.concatenate([sin_ref[...]] * reps, axis=1)
        o_ref[...] = (a * cos + _swap_half_heads(a) * sin).astype(o_ref.dtype)

    @pl.when(functools.reduce(jnp.logical_or, [j == t for t in PLAIN_TILES]))
    def _():
        o_ref[...] = plain().astype(o_ref.dtype)

    @pl.when(j == T_SMALL)
    def _():
        os_ref[...] = plain()

    @pl.when(j == T_LRU_X)
    def _():
        conv_then(lambda v: v)

    @pl.when(j == T_GDN_Q)
    def _():
        conv_then(lambda v: _l2_heads(_silu(v), GDN_DK, GDN_DK ** -0.5))

    @pl.when(j == T_GDN_K)
    def _():
        conv_then(lambda v: _l2_heads(_silu(v), GDN_DK, 1.0))

    @pl.when(j == T_GDN_V)
    def _():
        conv_then(_silu)

    @pl.when((j >= T_MERGE) & (j < T_SMALL))
    def _():
        o_ref[...] = _sigmoid(plain()).astype(o_ref.dtype)


def _proj_call(xa, scl, shl, scc, shc, w, b, cw, cb, cos, sin, *, nb, tb, ctx_len, tm):
    r, d = xa.shape
    n = w.shape[1]
    tpb = tb // tm
    kern = functools.partial(_proj_kernel, tm=tm, tpb=tpb, ctx_len=ctx_len, tb=tb)
    return pl.pallas_call(
        kern,
        out_shape=(jax.ShapeDtypeStruct((r, T_SMALL * TN), BF16), jax.ShapeDtypeStruct((r, TN), F32)),
        grid=(r // tm, n // TN),
        in_specs=_halo_specs(tm, d, r) + [
            pl.BlockSpec((1, 1, d), lambda i, j: (i // tpb, 0, 0)),
            pl.BlockSpec((1, 1, d), lambda i, j: (i // tpb, 0, 0)),
            pl.BlockSpec((1, d), lambda i, j: (0, 0)),
            pl.BlockSpec((1, d), lambda i, j: (0, 0)),
            pl.BlockSpec((d, TN), lambda i, j: (0, j)),
            pl.BlockSpec((1, TN), lambda i, j: (0, j)),
            pl.BlockSpec((LRU_CONV, TN), lambda i, j: (0, j)),
            pl.BlockSpec((1, TN), lambda i, j: (0, j)),
            pl.BlockSpec((tm, 128), lambda i, j: (i % tpb, 0)),
            pl.BlockSpec((tm, 128), lambda i, j: (i % tpb, 0)),
        ],
        out_specs=(pl.BlockSpec((tm, TN), lambda i, j: (i, jnp.minimum(j, T_SMALL - 1))),
                   pl.BlockSpec((tm, TN), lambda i, j: (i, 0))),
        scratch_shapes=[pltpu.VMEM((tm + 2 * HALO, d), BF16), pltpu.VMEM((tm + 2 * HALO, TN), F32)],
        compiler_params=_cparams(("arbitrary", "arbitrary")),
        name="proj",
    )(xa, xa, xa, scl, shl, scc, shc, w, b, cw, cb, cos, sin)


def _bwd_index(i, n_ctx, n_tot):
    return jnp.where(i < n_ctx, n_ctx - 1 - i, n_tot + n_ctx - 1 - i)


def _tri_masks(c):
    row = lax.broadcasted_iota(jnp.int32, (c, c), 0)
    col = lax.broadcasted_iota(jnp.int32, (c, c), 1)
    incl = (row >= col, row <= col)
    strict = (row > col, row < col)
    return row, col, incl, strict, tuple(jnp.where(m, 1.0, 0.0).astype(BF16) for m in incl)


def _gla_kernel(qkf_ref, vf_ref, gf_ref, qkb_ref, vb_ref, gb_ref, wg_ref, bg_ref, of_ref, ob_ref, st_ref, *, nb):
    c = CHUNK
    hk = GLA_HEADS * GLA_DK

    @pl.when(pl.program_id(0) == 0)
    def _():
        st_ref[...] = jnp.zeros_like(st_ref)

    _, _, incl, _, tri_bf = _tri_masks(c)
    refs = ((qkf_ref, vf_ref, gf_ref, of_ref), (qkb_ref, vb_ref, gb_ref, ob_ref))
    groups = [(b, d) for b in range(nb) for d in range(2)]
    z = [_mm3(refs[d][2][b, :, d * GLA_GATE_RANK:(d + 1) * GLA_GATE_RANK], wg_ref[d]) + bg_ref[d] for b, d in groups]
    la = [(jnp.minimum(v, 0.0) - jnp.log1p(jnp.exp(-jnp.abs(v)))) * (1.0 / GLA_GATE_NORM) for v in z]
    cum = [_cumdot(tri_bf[d], v) for (b, d), v in zip(groups, la)]
    last = [v[c - 1:c, :] if d == 0 else v[0:1, :] for (b, d), v in zip(groups, cum)]
    mid = [v[c // 2 - 1:c // 2, :] if d == 0 else v[c // 2:c // 2 + 1, :] for (b, d), v in zip(groups, cum)]
    q = [refs[d][0][b, :, :hk].astype(F32) for b, d in groups]
    k = [refs[d][0][b, :, hk:].astype(F32) for b, d in groups]
    vv = [refs[d][1][b].astype(BF16) for b, d in groups]
    n = range(len(groups))
    qd = [(q[g] * jnp.exp(cum[g] - mid[g])).astype(BF16) for g in n]
    km = [(k[g] * jnp.exp(mid[g] - cum[g])).astype(BF16) for g in n]
    kdec = [(k[g] * jnp.exp(last[g] - cum[g])).astype(BF16) for g in n]
    qs = [(q[g] * jnp.exp(cum[g])).astype(BF16) for g in n]
    dec = [jnp.exp(last[g]) for g in n]
    chains = [(g, h) for g in n for h in range(GLA_HEADS)]
    ksl = lambda h: slice(h * GLA_DK, (h + 1) * GLA_DK)
    vsl = lambda h: slice(h * GLA_DV, (h + 1) * GLA_DV)
    st = [st_ref[groups[g][0], groups[g][1], h] for g, h in chains]
    sc = [jnp.where(incl[groups[g][1]], _mm_nt(qd[g][:, ksl(h)], km[g][:, ksl(h)]), 0.0).astype(BF16)
          for g, h in chains]
    o_in = [_mm_nt(qs[g][:, ksl(h)], s.astype(BF16)) for (g, h), s in zip(chains, st)]
    o = [_mm(s, vv[g][:, vsl(h)]) + oi for (g, h), s, oi in zip(chains, sc, o_in)]
    ut = [_mm_tn(vv[g][:, vsl(h)], kdec[g][:, ksl(h)]) for g, h in chains]
    for (g, h), s, u in zip(chains, st, ut):
        st_ref[groups[g][0], groups[g][1], h] = s * dec[g][:, ksl(h)] + u
    for g, (b, d) in enumerate(groups):
        refs[d][3][b] = jnp.concatenate(o[g * GLA_HEADS:(g + 1) * GLA_HEADS], axis=1).astype(refs[d][3].dtype)


def _chunk_specs(nb, width, col, rowf):
    return pl.BlockSpec((nb, CHUNK, width), lambda i: (0, rowf(i), col))


def _gla_call(p3, ps3, wg, bg, *, nb, tb, ctx_len):
    nct, ncc = tb // CHUNK, ctx_len // CHUNK
    fwd = lambda i: i
    bwd = lambda i: _bwd_index(i, ncc, nct)
    width = GLA_HEADS * GLA_DV
    hk = GLA_HEADS * GLA_DK

    def specs(rowf):
        return [_chunk_specs(nb, TN, T_GLA_QK, rowf), _chunk_specs(nb, TN, T_GLA_V, rowf),
                _chunk_specs(nb, 128, 0, rowf)]
    out = jax.ShapeDtypeStruct((nb, tb, width), BF16)
    return pl.pallas_call(
        functools.partial(_gla_kernel, nb=nb),
        out_shape=(out, out),
        grid=(nct,),
        in_specs=specs(fwd) + specs(bwd) + [
            pl.BlockSpec((2, GLA_GATE_RANK, hk), lambda i: (0, 0, 0)),
            pl.BlockSpec((2, 1, hk), lambda i: (0, 0, 0))],
        out_specs=(_chunk_specs(nb, width, 0, fwd), _chunk_specs(nb, width, 0, bwd)),
        scratch_shapes=[pltpu.VMEM((nb, 2, GLA_HEADS, GLA_DV, GLA_DK), F32)],
        compiler_params=_cparams(("arbitrary",)),
        name="gla",
    )(p3, p3, ps3, p3, p3, ps3, wg, bg)


def _lru_kernel(x_ref, w_ref, gb_ref, lam_ref, o_ref, a_s, b_s, h_s, *, tl):
    d = pl.program_id(1)

    @pl.when(pl.program_id(2) == 0)
    def _():
        h_s[...] = jnp.zeros_like(h_s)

    xb = x_ref[...]
    x = xb.astype(F32)
    gt = _mm(xb, w_ref[0]) + gb_ref[0]
    r = _sigmoid(gt[:, :LRU_WIDTH])
    ig = _sigmoid(gt[:, LRU_WIDTH:])
    log_a = (-LRU_C) * r * _softplus(-lam_ref[0])
    a_s[...] = jnp.exp(log_a)
    b_s[...] = jnp.sqrt(1.0 - jnp.exp(2.0 * log_a)) * ig * x

    def body(s, h):
        rr = jnp.where(d == 0, s, tl - 1 - s)
        h = a_s[pl.ds(rr, 1), :] * h + b_s[pl.ds(rr, 1), :]
        b_s[pl.ds(rr, 1), :] = h
        return h
    h_s[...] = lax.fori_loop(0, tl, body, h_s[...], unroll=8)
    o_ref[0] = b_s[...].astype(o_ref.dtype)


def _lru_call(p, wbd, gb, lam, *, nb, tb, ctx_len, tl):
    r = p.shape[0]
    ntl, nctl = tb // tl, ctx_len // tl

    def rowblk(b, d, i):
        return b * ntl + jnp.where(d == 0, i, _bwd_index(i, nctl, ntl))
    return pl.pallas_call(
        functools.partial(_lru_kernel, tl=tl),
        out_shape=jax.ShapeDtypeStruct((2, r, LRU_WIDTH), BF16),
        grid=(nb, 2, ntl),
        in_specs=[pl.BlockSpec((tl, TN), lambda b, d, i: (rowblk(b, d, i), T_LRU_X)),
                  pl.BlockSpec((1, LRU_WIDTH, 2 * LRU_WIDTH), lambda b, d, i: (d, 0, 0)),
                  pl.BlockSpec((1, 1, 2 * LRU_WIDTH), lambda b, d, i: (d, 0, 0)),
                  pl.BlockSpec((1, 1, LRU_WIDTH), lambda b, d, i: (d, 0, 0))],
        out_specs=pl.BlockSpec((1, tl, LRU_WIDTH), lambda b, d, i: (d, rowblk(b, d, i), 0)),
        scratch_shapes=[pltpu.VMEM((tl, LRU_WIDTH), F32), pltpu.VMEM((tl, LRU_WIDTH), F32),
                        pltpu.VMEM((1, LRU_WIDTH), F32)],
        compiler_params=_cparams(("arbitrary", "arbitrary", "arbitrary")),
        name="lru",
    )(p, wbd, gb, lam)


def _mmb(a, b):
    return _mm(a.astype(BF16), b.astype(BF16))


def _unit_tri_inverses(a_list, row, col):
    eye = jnp.where(row == col, 1.0, 0.0)
    blk = lambda s: (row // s) == (col // s)
    in8 = blk(8)
    a0 = [jnp.where(in8, a, 0.0) for a in a_list]
    p2 = [_mmb(x, x) for x in a0]
    ia = [eye - x for x in a0]
    t = [x + _mmb(x, p) for x, p in zip(ia, p2)]
    p4 = [_mmb(p, p) for p in p2]
    t = [x + _mmb(x, p) for x, p in zip(t, p4)]
    s = 8
    while s < CHUNK:
        sel = blk(2 * s) & jnp.logical_not(blk(s))
        y = [_mmb(jnp.where(sel, a, 0.0), x) for a, x in zip(a_list, t)]
        t = [x - _mmb(x, v) for x, v in zip(t, y)]
        s *= 2
    return t


def _gdn_kernel(qkvf_ref, gf_ref, qkvb_ref, gb_ref, alog_ref, dtb_ref, of_ref, ob_ref, s_ref, *, nb):
    c = CHUNK
    hk = GDN_HEADS * GDN_DK

    @pl.when(pl.program_id(0) == 0)
    def _():
        s_ref[...] = jnp.zeros_like(s_ref)

    row, col, incl, strict, tri_bf = _tri_masks(c)
    refs = ((qkvf_ref, gf_ref, of_ref), (qkvb_ref, gb_ref, ob_ref))
    neg_a = -jnp.exp(alog_ref[...])
    groups = [(b, d) for b in range(nb) for d in range(2)]
    gblk = [refs[d][1][b] for b, d in groups]
    g_all = [neg_a * _softplus(v + dtb_ref[...]) for v in gblk]
    beta_all = [_sigmoid(v) for v in gblk]
    gc_all = [_cumdot(tri_bf[d], v) for (b, d), v in zip(groups, g_all)]
    gc_t = [v.T for v in gc_all]

    chains = [(g, h) for g in range(len(groups)) for h in range(GDN_HEADS)]
    dirn = lambda g: groups[g][1]
    lane_a = lambda g, h: SM_A + dirn(g) * GDN_HEADS + h
    lane_b = lambda g, h: SM_B + dirn(g) * GDN_HEADS + h
    gcol = [gc_all[g][:, lane_a(g, h):lane_a(g, h) + 1] for g, h in chains]
    grow = [gc_t[g][lane_a(g, h):lane_a(g, h) + 1, :] for g, h in chains]
    beta = [beta_all[g][:, lane_b(g, h):lane_b(g, h) + 1] for g, h in chains]
    glast = [v[c - 1:c, :] if dirn(g) == 0 else v[0:1, :] for (g, h), v in zip(chains, gcol)]
    decay = [jnp.exp(jnp.where(incl[dirn(g)], gc - gr, -jnp.inf)) for (g, h), gc, gr in zip(chains, gcol, grow)]

    def head(g, h, part):
        b, d = groups[g]
        return refs[d][0][b, :, part * hk + h * GDN_DK:part * hk + (h + 1) * GDN_DK]
    k_bf = [head(g, h, 1) for g, h in chains]
    qh = [head(g, h, 0).astype(F32) for g, h in chains]
    kh = [k.astype(F32) for k in k_bf]
    vh = [head(g, h, 2).astype(F32) for g, h in chains]
    kb = [k * bt for k, bt in zip(kh, beta)]
    a = [jnp.where(strict[dirn(g)], _mm_nt(x.astype(BF16), k) * dc, 0.0)
         for (g, h), x, k, dc in zip(chains, kb, k_bf, decay)]
    attn = [jnp.where(incl[dirn(g)], _mm_nt(x.astype(BF16), k) * dc, 0.0).astype(BF16)
            for (g, h), x, k, dc in zip(chains, qh, k_bf, decay)]
    t = _unit_tri_inverses(a, row, col)
    egc = [jnp.exp(v) for v in gcol]
    sol = [_mmb(x, jnp.concatenate([v * bt, k * e], axis=1)) for x, v, bt, k, e in zip(t, vh, beta, kb, egc)]
    kdec = [(k * jnp.exp(gl - gc)).astype(BF16) for k, gl, gc in zip(kh, glast, gcol)]
    qdec = [(x * e).astype(BF16) for x, e in zip(qh, egc)]
    s = [s_ref[groups[g][0], groups[g][1], h] for g, h in chains]
    s_bf = [v.astype(BF16) for v in s]
    v_new = [x[:, :GDN_DV] - _mm(x[:, GDN_DV:].astype(BF16), sb) for x, sb in zip(sol, s_bf)]
    vn_bf = [v.astype(BF16) for v in v_new]
    o = [_mm(x, sb) + _mm(at, vn) for x, sb, at, vn in zip(qdec, s_bf, attn, vn_bf)]
    for (g, h), sv, gl, kd, vn in zip(chains, s, glast, kdec, vn_bf):
        s_ref[groups[g][0], groups[g][1], h] = sv * jnp.exp(gl) + _mm_tn(kd, vn)
    for g, (b, d) in enumerate(groups):
        refs[d][2][b] = jnp.concatenate(o[g * GDN_HEADS:(g + 1) * GDN_HEADS], axis=1).astype(refs[d][2].dtype)


def _gdn_call(p3, ps3, alog, dtb, *, nb, tb, ctx_len):
    nct, ncc = tb // CHUNK, ctx_len // CHUNK
    fwd = lambda i: i
    bwd = lambda i: _bwd_index(i, ncc, nct)
    width = GDN_HEADS * GDN_DV
    qkv_w = 3 * TN

    def specs(rowf):
        return [_chunk_specs(nb, qkv_w, T_GDN_Q * TN // qkv_w, rowf), _chunk_specs(nb, 128, 0, rowf)]
    out = jax.ShapeDtypeStruct((nb, tb, width), BF16)
    return pl.pallas_call(
        functools.partial(_gdn_kernel, nb=nb),
        out_shape=(out, out),
        grid=(nct,),
        in_specs=specs(fwd) + specs(bwd) + [pl.BlockSpec((1, 128), lambda i: (0, 0)),
                                            pl.BlockSpec((1, 128), lambda i: (0, 0))],
        out_specs=(_chunk_specs(nb, width, 0, fwd), _chunk_specs(nb, width, 0, bwd)),
        scratch_shapes=[pltpu.VMEM((nb, 2, GDN_HEADS, GDN_DK, GDN_DV), F32)],
        compiler_params=_cparams(("arbitrary",)),
        name="gdn",
    )(p3, ps3, p3, ps3, alog, dtb)


def _na_kernel(q_ref, k0_ref, k1_ref, k2_ref, v0_ref, v1_ref, v2_ref, kc_ref, vc_ref, bias_ref, o_ref,
               kbuf, vbuf, *, rows):
    jj = pl.program_id(1)
    tq = NA_ROWS * GRID_W
    hd = NA_HD

    @pl.when(jj == 0)
    def _():
        outs = []
        for h in range(NA_HEADS):
            sl = slice(h * hd, (h + 1) * hd)
            s = _mm_nt(q_ref[:, sl], k1_ref[:, sl])
            m = jnp.max(s, axis=-1, keepdims=True)
            e = jnp.exp(s - m)
            l = jnp.sum(e, axis=-1, keepdims=True)
            outs.append(_mm(e.astype(BF16), v1_ref[:, sl]) / l)
        o_ref[...] = jnp.concatenate(outs, axis=1).astype(o_ref.dtype)

    @pl.when(jj > 0)
    def _():
        jp = jj - 1
        for n, (kr, vr) in enumerate(((k0_ref, v0_ref), (k1_ref, v1_ref), (k2_ref, v2_ref))):
            kbuf[n * tq:(n + 1) * tq, :] = kr[...]
            vbuf[n * tq:(n + 1) * tq, :] = vr[...]
        kc = kc_ref[...]
        vc = vc_ref[...]
        for a in range(NA_ROWS):
            r = NA_ROWS * jp + a
            rs = jnp.clip(r - NA_WIN_R // 2, 0, rows - NA_WIN_R)
            start = pl.multiple_of((rs - NA_ROWS * (jp - 1)) * GRID_W, GRID_W)
            e_idx = r - rs
            heads = range(NA_HEADS)
            sl = [slice(h * hd, (h + 1) * hd) for h in heads]
            win = pl.ds(start, NA_WIN_R * GRID_W)
            qh = [q_ref[a * GRID_W:(a + 1) * GRID_W, sl[h]] for h in heads]
            s_loc = [_mm_nt(qh[h], kbuf[win, sl[h]]) + bias_ref[e_idx, h] for h in heads]
            s_ctx = [_mm_nt(qh[h], kc[:, sl[h]]) for h in heads]
            m = [jnp.maximum(jnp.max(s_loc[h], axis=-1, keepdims=True), jnp.max(s_ctx[h], axis=-1, keepdims=True))
                 for h in heads]
            p_loc = [jnp.exp(s_loc[h] - m[h]) for h in heads]
            p_ctx = [jnp.exp(s_ctx[h] - m[h]) for h in heads]
            l = [jnp.sum(p_loc[h], axis=-1, keepdims=True) + jnp.sum(p_ctx[h], axis=-1, keepdims=True) for h in heads]
            o = [(_mm(p_loc[h].astype(BF16), vbuf[win, sl[h]]) + _mm(p_ctx[h].astype(BF16), vc[:, sl[h]])) / l[h]
                 for h in heads]
            o_ref[a * GRID_W:(a + 1) * GRID_W, :] = jnp.concatenate(o, axis=1).astype(o_ref.dtype)


def _na_call(p, bias, *, nb, tb, ctx_len):
    r = p.shape[0]
    tq = NA_ROWS * GRID_W
    assert ctx_len == tq, "context block must be exactly one query block"
    nblk = tb // tq
    nlb = nblk - 1
    rows = (tb - ctx_len) // GRID_W
    assert rows >= NA_WIN_R and rows % NA_ROWS == 0
    width = NA_HEADS * NA_HD
    cur = lambda b, j: b * nblk + j
    prv = lambda b, j: b * nblk + 1 + jnp.clip(j - 2, 0, nlb - 1)
    nxt = lambda b, j: b * nblk + 1 + jnp.clip(j, 0, nlb - 1)
    blk = lambda rowf, t: pl.BlockSpec((tq, TN), lambda b, j: (rowf(b, j), t))
    return pl.pallas_call(
        functools.partial(_na_kernel, rows=rows),
        out_shape=jax.ShapeDtypeStruct((r, width), BF16),
        grid=(nb, nblk),
        in_specs=[blk(cur, T_NA_Q),
                  blk(prv, T_NA_K), blk(cur, T_NA_K), blk(nxt, T_NA_K),
                  blk(prv, T_NA_V), blk(cur, T_NA_V), blk(nxt, T_NA_V),
                  pl.BlockSpec((ctx_len, TN), lambda b, j: (b * nblk, T_NA_K)),
                  pl.BlockSpec((ctx_len, TN), lambda b, j: (b * nblk, T_NA_V)),
                  pl.BlockSpec(bias.shape, lambda b, j: (0, 0, 0, 0), pipeline_mode=pl.Buffered(1))],
        out_specs=pl.BlockSpec((tq, width), lambda b, j: (cur(b, j), 0)),
        scratch_shapes=[pltpu.VMEM((3 * tq, TN), BF16), pltpu.VMEM((3 * tq, TN), BF16)],
        compiler_params=_cparams(("arbitrary", "arbitrary")),
        name="na",
    )(p, p, p, p, p, p, p, p, p, bias)


def _na_bias_table(rpb):
    c = np.arange(GRID_W)
    cs = np.clip(c - NA_WIN_C // 2, 0, GRID_W - NA_WIN_C)
    kc = np.arange(GRID_W)
    inwin = (kc[None, :] >= cs[:, None]) & (kc[None, :] < cs[:, None] + NA_WIN_C)
    dc = np.clip(kc[None, :] - c[:, None] + NA_WIN_C - 1, 0, 2 * NA_WIN_C - 2)
    onehot = np.zeros((2 * NA_WIN_C - 1, GRID_W * GRID_W), np.float32)
    onehot[dc.reshape(-1), np.arange(GRID_W * GRID_W)] = 1.0
    byrow = jnp.einsum('hrd,dm->hrm', rpb, onehot, precision=lax.Precision.HIGHEST)
    byrow = jnp.where(inwin[None, None], byrow.reshape(NA_HEADS, 2 * NA_WIN_R - 1, GRID_W, GRID_W), NEG_BIG)
    tabs = [jnp.transpose(byrow[:, NA_WIN_R - 1 - e:2 * NA_WIN_R - 1 - e], (0, 2, 1, 3))
            .reshape(NA_HEADS, GRID_W, NA_WIN_R * GRID_W) for e in range(NA_WIN_R)]
    return jnp.stack(tabs)


def _head_rms(o, g, width):
    outs = []
    for h in range(o.shape[1] // width):
        xs = o[:, h * width:(h + 1) * width]
        ms = jnp.mean(xs * xs, axis=-1, keepdims=True)
        outs.append(xs * lax.rsqrt(ms + NORM_EPS) * g)
    return jnp.concatenate(outs, axis=1)


def _merge_kernel(gof_ref, gob_ref, gg_ref, lhf_ref, lhb_ref, ly_ref, dof_ref, dob_ref, dz_ref, na_ref,
                  m0_ref, m1_ref, m2_ref, m3_ref, x_ref, g1l_ref, g1c_ref, wbr_ref, wout_ref, bout_ref,
                  gng_ref, gnd_ref, lng_ref, lnb_ref, o_ref, *, tm, tpb, ctx_len, alpha):
    p0 = (pl.program_id(0) % tpb) * tm
    f = lambda ref: ref[...].astype(F32)
    y_a = _head_rms(f(gof_ref) + f(gob_ref), gng_ref[...], GLA_DV) * _silu(f(gg_ref))
    y_b = (lhf_ref[0].astype(F32) + lhb_ref[0].astype(F32)) * _gelu_tanh(f(ly_ref))
    y_c = _head_rms(f(dof_ref) + f(dob_ref), gnd_ref[...], GDN_DV) * _silu(f(dz_ref))
    y_d = na_ref[...]
    m = None
    for n, (y, g_ref) in enumerate(((y_a, m0_ref), (y_b, m1_ref), (y_c, m2_ref), (y_d, m3_ref))):
        term = g_ref[...].astype(F32) * _mm(y.astype(BF16), wbr_ref[n])
        m = term if m is None else m + term
    out = _mm(m.astype(BF16), wout_ref[...]) + bout_ref[...]
    is_ctx = _row_pos(tm, p0) < ctx_len
    g1 = jnp.where(is_ctx, g1c_ref[...], g1l_ref[0])
    o_ref[...] = _layer_norm(alpha * x_ref[...] + g1 * out, lng_ref[...], lnb_ref[...])


def _merge_call(p, gof, gob, lh, dof, dob, na, xa, g1l, g1c, wbr, wout, bout, gng, gnd, lng, lnb,
                *, tb, ctx_len, tm, alpha):
    r, d = xa.shape
    tpb = tb // tm
    bw = BRANCH_W
    row = lambda w: pl.BlockSpec((tm, w), lambda i: (i, 0))
    ptile = lambda t, w: pl.BlockSpec((tm, w), lambda i: (i, t * TN // w))
    const = lambda a: pl.BlockSpec(a.shape, lambda i: (0,) * a.ndim)
    kern = functools.partial(_merge_kernel, tm=tm, tpb=tpb, ctx_len=ctx_len, alpha=alpha)
    return pl.pallas_call(
        kern,
        out_shape=jax.ShapeDtypeStruct((r, d), F32),
        grid=(r // tm,),
        in_specs=[row(bw), row(bw), ptile(T_GLA_G, bw),
                  pl.BlockSpec((1, tm, bw), lambda i: (0, i, 0)), pl.BlockSpec((1, tm, bw), lambda i: (1, i, 0)),
                  ptile(T_LRU_Y, bw),
                  row(bw), row(bw), ptile(T_GDN_Z, bw),
                  row(bw),
                  pl.BlockSpec((tm, d), lambda i: (i, T_MERGE * TN // d + 0)),
                  pl.BlockSpec((tm, d), lambda i: (i, T_MERGE * TN // d + 1)),
                  pl.BlockSpec((tm, d), lambda i: (i, T_MERGE * TN // d + 2)),
                  pl.BlockSpec((tm, d), lambda i: (i, T_MERGE * TN // d + 3)),
                  row(d),
                  pl.BlockSpec((1, 1, d), lambda i: (i // tpb, 0, 0)),
                  const(g1c), const(wbr), const(wout), const(bout), const(gng), const(gnd), const(lng), const(lnb)],
        out_specs=row(d),
        compiler_params=_cparams(("arbitrary",)),
        name="merge",
    )(gof, gob, p, lh, lh, p, dof, dob, p, na, p, p, p, p, xa, g1l, g1c, wbr, wout, bout, gng, gnd, lng, lnb)


def _ffn_kernel(x_ref, xp_ref, xn_ref, scl_ref, shl_ref, gl_ref, scc_ref, shc_ref, gc_ref, wa_ref, wb_ref,
                ba_ref, bb_ref, cw_ref, cb_ref, wd_ref, bd_ref, lng_ref, lnb_ref, o_ref,
                lhs_ref, a_ref, h_ref, *, tm, tpb, ctx_len, tb, nc, tc, alpha):
    i = pl.program_id(0)
    c = pl.program_id(1)
    p0 = (i % tpb) * tm

    @pl.when(c == 0)
    def _():
        _build_lhs(lhs_ref, x_ref, xp_ref, xn_ref, scl_ref[0], shl_ref[0], scc_ref[...], shc_ref[...],
                   p0, tm, ctx_len)

    a_ref[...] = _mm(lhs_ref[...], wa_ref[...]) + ba_ref[...]
    gate = _mm(lhs_ref[HALO:HALO + tm, :], wb_ref[...]) + bb_ref[...]
    left = FFN_CONV // 2
    hidden = (_silu(_dwconv(a_ref, cw_ref, left, 0, tm) + cb_ref[...]) * gate).astype(BF16)
    fixes = [(r0, (_silu(_dwconv(a_ref, cw_ref, left, r0, HALO, p0, ctx_len, tb) + cb_ref[...])
                   * gate[r0:r0 + HALO, :]).astype(BF16)) for r0 in _segment_end_windows(tm, ctx_len)]
    for cc in range(nc):
        @pl.when(c == cc)
        def _(cc=cc):
            h_ref[:, cc * tc:(cc + 1) * tc] = hidden
            for r0, fixed in fixes:
                h_ref[r0:r0 + HALO, cc * tc:(cc + 1) * tc] = fixed

    @pl.when(c == nc - 1)
    def _():
        f = _mm(h_ref[...], wd_ref[...]) + bd_ref[...]
        is_ctx = _row_pos(tm, p0) < ctx_len
        g2 = jnp.where(is_ctx, gc_ref[...], gl_ref[0])
        o_ref[...] = _layer_norm(alpha * x_ref[...] + g2 * f, lng_ref[...], lnb_ref[...])


def _ffn_call(xa, scl, shl, gl, scc, shc, gc, w_up, b_up, cw, cb, w_down, b_down, lng, lnb,
              *, tb, ctx_len, tm, tc, alpha):
    r, d = xa.shape
    dff = w_down.shape[0]
    nc = dff // tc
    tpb = tb // tm
    lat = pl.BlockSpec((1, 1, d), lambda i, c: (i // tpb, 0, 0))
    vec = pl.BlockSpec((1, d), lambda i, c: (0, 0))
    kern = functools.partial(_ffn_kernel, tm=tm, tpb=tpb, ctx_len=ctx_len, tb=tb, nc=nc, tc=tc, alpha=alpha)
    return pl.pallas_call(
        kern,
        out_shape=jax.ShapeDtypeStruct((r, d), F32),
        grid=(r // tm, nc),
        in_specs=_halo_specs(tm, d, r) + [
            lat, lat, lat, vec, vec, vec,
            pl.BlockSpec((d, tc), lambda i, c: (0, c)),
            pl.BlockSpec((d, tc), lambda i, c: (0, nc + c)),
            pl.BlockSpec((1, tc), lambda i, c: (0, c)),
            pl.BlockSpec((1, tc), lambda i, c: (0, nc + c)),
            pl.BlockSpec((FFN_CONV, tc), lambda i, c: (0, c)),
            pl.BlockSpec((1, tc), lambda i, c: (0, c)),
            pl.BlockSpec((dff, d), lambda i, c: (0, 0), pipeline_mode=pl.Buffered(1)),
            vec, vec, vec],
        out_specs=pl.BlockSpec((tm, d), lambda i, c: (i, 0)),
        scratch_shapes=[pltpu.VMEM((tm + 2 * HALO, d), BF16), pltpu.VMEM((tm + 2 * HALO, tc), F32),
                        pltpu.VMEM((tm, dff), BF16)],
        compiler_params=_cparams(("arbitrary", "arbitrary")),
        name="ffn",
    )(xa, xa, xa, scl, shl, gl, scc, shc, gc, w_up, w_up, b_up, b_up, cw, cb, w_down, b_down, lng, lnb)


def _pad_cols(a, width):
    return jnp.pad(a, ((0, 0), (0, width - a.shape[1])))


def _in_proj_params(w_in, b_in, lru_conv_w, lru_conv_b, gdn_conv_w):
    sizes = (GLA_HEADS * GLA_DK, GLA_HEADS * GLA_DK, GLA_HEADS * GLA_DV, GLA_HEADS * GLA_DV,
             GLA_GATE_RANK, GLA_GATE_RANK, LRU_WIDTH, LRU_WIDTH,
             GDN_HEADS * GDN_DK, GDN_HEADS * GDN_DK, GDN_HEADS * GDN_DV, GDN_HEADS * GDN_DV,
             2 * GDN_HEADS, 2 * GDN_HEADS, NA_HEADS * NA_HD, NA_HEADS * NA_HD, NA_HEADS * NA_HD)
    names = ('gla_q', 'gla_k', 'gla_v', 'gla_g', 'gla_fw', 'gla_bw', 'lru_x', 'lru_y', 'gdn_q', 'gdn_k',
             'gdn_v', 'gdn_z', 'gdn_a', 'gdn_b', 'na_q', 'na_k', 'na_v')
    offs = np.concatenate([[0], np.cumsum(sizes)])
    wb = jnp.concatenate([w_in, b_in[None, :]], axis=0)
    col = {n: wb[:, offs[k]:offs[k + 1]] for k, n in enumerate(names)}
    col['merge'] = wb[:, offs[-1]:]
    small = _pad_cols(jnp.concatenate([col['gla_fw'], col['gla_bw'], col['gdn_a'], col['gdn_b']], axis=1), TN)
    tiles = [jnp.concatenate([col['gla_q'] * (GLA_DK ** -0.5), col['gla_k']], axis=1),
             col['gla_v'], col['gla_g'],
             col['gdn_q'], col['gdn_k'], col['gdn_v'], col['gdn_z'], col['lru_x'], col['lru_y'],
             col['na_q'] * (NA_HD ** -0.5), col['na_k'], col['na_v'], col['merge'], small]
    wb = jnp.concatenate(tiles, axis=1)
    n = wb.shape[1]
    assert n == N_TILES * TN
    cw = jnp.zeros((LRU_CONV, n), F32)
    cw = cw.at[:, T_LRU_X * TN:(T_LRU_X + 1) * TN].set(lru_conv_w)
    cw = cw.at[:, T_GDN_Q * TN:(T_GDN_V + 1) * TN].set(gdn_conv_w)
    cb = jnp.zeros((1, n), F32).at[0, T_LRU_X * TN:(T_LRU_X + 1) * TN].set(lru_conv_b)
    return wb[:-1].astype(BF16), wb[-1:], cw, cb


def _rope_tables(ctx_len, seq_len):
    quarter = GLA_DK // 4
    t = jnp.arange(seq_len)
    inv = ROPE_BASE ** (-jnp.arange(quarter, dtype=F32) / quarter)
    ang = jnp.concatenate([(t // GRID_W).astype(F32)[:, None] * inv,
                           (t % GRID_W).astype(F32)[:, None] * inv], -1)
    cos = jnp.concatenate([jnp.ones((ctx_len, 2 * quarter), F32), jnp.cos(ang)], axis=0)
    sin = jnp.concatenate([jnp.zeros((ctx_len, 2 * quarter), F32), jnp.sin(ang)], axis=0)
    return jnp.concatenate([cos, cos, cos, cos], axis=1), jnp.concatenate([-sin, sin, -sin, sin], axis=1)


def _lru_gate_matrix(gate_w):
    eye = jnp.eye(LRU_BLOCKS, dtype=F32)
    dense = jnp.einsum('xgnde,nm->xndgme', gate_w, eye)
    return dense.reshape(2, LRU_WIDTH, 2 * LRU_WIDTH).astype(BF16)


def _small_lane_vec(v):
    return jnp.zeros((1, 128), F32).at[0, SM_A:SM_A + 2 * GDN_HEADS].set(v.reshape(-1))


def _row_tile(tb, target):
    best = HALO
    for t in range(HALO, target + 1, HALO):
        if tb % t == 0:
            best = t
    return best


def kernel(x, c, ctx, c_ctx, w_mod, b_mod, w_in, b_in, gla_w_gate, gla_b_gate, gla_norm, lru_conv_w, lru_conv_b,
           lru_gate_w, lru_gate_b, lru_lambda, gdn_conv_w, gdn_a_log, gdn_dt_bias, gdn_norm, na_rpb, w_branch,
           w_out, b_out, ln1_g, ln1_b, ffn_w_up, ffn_b_up, ffn_conv_w, ffn_conv_b, ffn_w_down, ffn_b_down,
           ln2_g, ln2_b):
    nb, seq_len, d = x.shape
    ctx_len = ctx.shape[1]
    depth = w_mod.shape[0]
    tb = ctx_len + seq_len
    alpha = (2 * depth) ** 0.25
    tm = _row_tile(tb, ROW_TILE)
    tl = NA_ROWS * GRID_W
    dims = dict(nb=nb, tb=tb, ctx_len=ctx_len)

    xa = jnp.concatenate([ctx, x], axis=1).reshape(nb * tb, d)
    crows = jnp.zeros((8, d), F32).at[:nb].set(c).at[nb].set(c_ctx)
    mods = _mod_call(crows, w_mod, b_mod)
    cos, sin = _rope_tables(ctx_len, seq_len)

    for l in range(depth):
        lat = [mods[l, :nb, k * d:(k + 1) * d].reshape(nb, 1, d) for k in range(6)]
        cx = [mods[l, nb:nb + 1, k * d:(k + 1) * d] for k in range(6)]
        w, b, cw, cb = _in_proj_params(w_in[l], b_in[l], lru_conv_w[l], lru_conv_b[l], gdn_conv_w[l])
        p, ps = _proj_call(xa, lat[1], lat[0], cx[1], cx[0], w, b, cw, cb, cos, sin, tm=tm, **dims)

        p3, ps3 = p.reshape(nb, tb, -1), ps.reshape(nb, tb, -1)
        flat = lambda a: a.reshape(nb * tb, -1)
        gof, gob = map(flat, _gla_call(p3, ps3, gla_w_gate[l], gla_b_gate[l].reshape(2, 1, -1), **dims))
        lh = _lru_call(p, _lru_gate_matrix(lru_gate_w[l]), lru_gate_b[l].reshape(2, 1, -1),
                       lru_lambda[l].reshape(2, 1, -1), tl=tl, **dims)
        dof, dob = map(flat, _gdn_call(p3, ps3, _small_lane_vec(gdn_a_log[l]), _small_lane_vec(gdn_dt_bias[l]),
                                       **dims))
        na = _na_call(p, _na_bias_table(na_rpb[l]), **dims)

        xa = _merge_call(p, gof, gob, lh, dof, dob, na, xa, lat[2], cx[2],
                         w_branch[l].astype(BF16), w_out[l].astype(BF16), b_out[l].reshape(1, d),
                         gla_norm[l].reshape(1, -1),
                         gdn_norm[l].reshape(1, -1), ln1_g[l].reshape(1, d), ln1_b[l].reshape(1, d),
                         tb=tb, ctx_len=ctx_len, tm=tl, alpha=alpha)
        xa = _ffn_call(xa, lat[4], lat[3], lat[5], cx[4], cx[3], cx[5],
                       ffn_w_up[l].astype(BF16), ffn_b_up[l].reshape(1, -1), ffn_conv_w[l],
                       ffn_conv_b[l].reshape(1, -1), ffn_w_down[l].astype(BF16), ffn_b_down[l].reshape(1, d),
                       ln2_g[l].reshape(1, d), ln2_b[l].reshape(1, d),
                       tb=tb, ctx_len=ctx_len, tm=tm, tc=256, alpha=alpha)
    return xa.reshape(nb, tb, d)[:, ctx_len:, :]
```

```python
import functools
import math

import jax
import jax.numpy as jnp
import numpy as np
from jax import lax
from jax.experimental import pallas as pl
from jax.experimental.pallas import tpu as pltpu

F32 = jnp.float32
BF16 = jnp.bfloat16

GRID_W = 64
CHUNK = 64
GLA_HEADS, GLA_DK, GLA_DV = 4, 64, 128
GLA_GATE_RANK = 16
GLA_GATE_NORM = 16.0
ROPE_BASE = 10000.0
LRU_WIDTH, LRU_BLOCKS, LRU_CONV, LRU_C = 512, 8, 4, 8.0
GDN_HEADS, GDN_DK, GDN_DV, GDN_CONV = 4, 128, 128, 4
NA_HEADS, NA_HD, NA_WIN_R, NA_WIN_C = 8, 64, 8, 16
FFN_CONV = 3
LN_EPS = 1e-5
NORM_EPS = 1e-6
BRANCH_W = 512

HALO = 16
ROW_TILE = 1280
ROW_SUB = 4
TN = 512
NA_ROWS = 4
VMEM_LIMIT = 56 * 1024 * 1024
NEG_BIG = -1e30

T_GLA_QK, T_GLA_V, T_GLA_G = 0, 1, 2
T_GDN_Q, T_GDN_K, T_GDN_V, T_GDN_Z = 3, 4, 5, 6
T_LRU_X, T_LRU_Y = 7, 8
T_NA_Q, T_NA_K, T_NA_V = 9, 10, 11
T_MERGE = 12
T_SMALL = 20
N_TILES = 21
PLAIN_TILES = (T_GLA_V, T_GLA_G, T_GDN_Z, T_LRU_Y, T_NA_Q, T_NA_K, T_NA_V)
assert T_SMALL == N_TILES - 1
SM_FW, SM_BW, SM_A, SM_B = 0, 16, 32, 40


def _cparams(sem):
    return pltpu.CompilerParams(dimension_semantics=sem, vmem_limit_bytes=VMEM_LIMIT)


def _mm(a, b):
    return jnp.dot(a, b, preferred_element_type=F32)


def _mm_nt(a, b):
    return lax.dot_general(a, b, (((1,), (1,)), ((), ())), preferred_element_type=F32)


def _mm_tn(a, b):
    return lax.dot_general(a, b, (((0,), (0,)), ((), ())), preferred_element_type=F32)


def _split2(a):
    hi = a.astype(BF16)
    lo = (a - hi.astype(F32)).astype(BF16)
    return hi, lo


def _mm3(a, b):
    ah, al = _split2(a)
    bh, bl = _split2(b)
    return _mm(ah, bh) + (_mm(ah, bl) + _mm(al, bh))


def _cumdot(tri_bf, x):
    h1 = x.astype(BF16)
    r1 = x - h1.astype(F32)
    h2 = r1.astype(BF16)
    h3 = (r1 - h2.astype(F32)).astype(BF16)
    return _mm(tri_bf, h1) + (_mm(tri_bf, h2) + _mm(tri_bf, h3))


def _sigmoid(x):
    return 0.5 * jnp.tanh(0.5 * x) + 0.5


def _silu(x):
    return x * _sigmoid(x)


def _softplus(x):
    return jnp.maximum(x, 0.0) + jnp.log1p(jnp.exp(-jnp.abs(x)))


def _gelu_tanh(x):
    return x * (0.5 * (1.0 + jnp.tanh(math.sqrt(2.0 / math.pi) * (x + 0.044715 * (x * x * x)))))


def _layer_norm(y, g, b):
    mu = jnp.mean(y, axis=-1, keepdims=True)
    yc = y - mu
    var = jnp.mean(yc * yc, axis=-1, keepdims=True)
    return yc * lax.rsqrt(var + LN_EPS) * g + b


def _row_pos(n, off):
    return lax.broadcasted_iota(jnp.int32, (n, 1), 0) + off


def _mod_kernel(c_ref, w_ref, b_ref, o_ref):
    cv = c_ref[...]
    o_ref[0] = jnp.dot(_silu(cv), w_ref[0], precision=lax.Precision.HIGHEST,
                       preferred_element_type=F32) + b_ref[0]


def _mod_call(crows, w_mod, b_mod):
    depth, d, n = w_mod.shape
    tn = n // 4
    return pl.pallas_call(
        _mod_kernel,
        out_shape=jax.ShapeDtypeStruct((depth, 8, n), F32),
        grid=(depth, n // tn),
        in_specs=[pl.BlockSpec((8, d), lambda l, j: (0, 0)),
                  pl.BlockSpec((1, d, tn), lambda l, j: (l, 0, j)),
                  pl.BlockSpec((1, 1, tn), lambda l, j: (l, 0, j))],
        out_specs=pl.BlockSpec((1, 8, tn), lambda l, j: (l, 0, j)),
        compiler_params=_cparams(("arbitrary", "arbitrary")),
        name="mod",
    )(crows, w_mod, b_mod.reshape(depth, 1, n))


def _build_lhs(lhs_ref, x_ref, xp_ref, xn_ref, scl, shl, scc, shc, p0, tm, ctx_len):
    def mod(xv, p):
        is_ctx = (p >= 0) & (p < ctx_len)
        sc = jnp.where(is_ctx, scc, scl)
        sh = jnp.where(is_ctx, shc, shl)
        return (xv * (1.0 + sc) + sh).astype(BF16)
    lhs_ref[0:HALO, :] = mod(xp_ref[...], _row_pos(HALO, p0 - HALO))
    lhs_ref[HALO:HALO + tm, :] = mod(x_ref[...], _row_pos(tm, p0))
    lhs_ref[HALO + tm:, :] = mod(xn_ref[...], _row_pos(HALO, p0 + tm))


def _dwconv(acc_ref, cw_ref, left, a0, n, pos=None, ctx_len=None, tb=None):
    if pos is not None:
        p = _row_pos(n, pos)
        in_ctx = p < ctx_len
        seg_lo = jnp.where(in_ctx, 0, ctx_len)
        seg_hi = jnp.where(in_ctx, ctx_len, tb)
    out = None
    for k in range(cw_ref.shape[0]):
        d = k - left
        v = acc_ref[pl.ds(a0 + d, n), :]
        if pos is not None:
            v = jnp.where((p + d >= seg_lo) & (p + d < seg_hi), v, 0.0)
        term = v * cw_ref[k:k + 1, :]
        out = term if out is None else out + term
    return out


def _sub_blocks(tm):
    n = tm // ROW_SUB if tm % (ROW_SUB * HALO) == 0 else tm
    return [(s0, n) for s0 in range(0, tm, n)]


def _acc_row(r, tm):
    n = _sub_blocks(tm)[0][1]
    return (r // n) * (n + 2 * HALO) + HALO + r % n


def _segment_end_windows(tm, ctx_len):
    starts = {0, tm - HALO}
    if ctx_len % tm:
        starts |= {ctx_len % tm - HALO, ctx_len % tm}
    return sorted(starts)


def _halo_specs(tm, d, r):
    hb = tm // HALO
    return [pl.BlockSpec((tm, d), lambda i, j: (i, 0)),
            pl.BlockSpec((HALO, d), lambda i, j: (jnp.maximum(i * hb - 1, 0), 0)),
            pl.BlockSpec((HALO, d), lambda i, j: (jnp.minimum((i + 1) * hb, r // HALO - 1), 0))]


def _swap_half_heads(x):
    n = x.shape[1]
    lane = lax.broadcasted_iota(jnp.int32, x.shape, 1)
    return jnp.where((lane % 64) < 32, pltpu.roll(x, n - 32, 1), pltpu.roll(x, 32, 1))


def _l2_heads(x, width, scale):
    outs = []
    for h in range(x.shape[1] // width):
        xs = x[:, h * width:(h + 1) * width]
        ss = jnp.sum(xs * xs, axis=-1, keepdims=True)
        y = xs * lax.rsqrt(ss + NORM_EPS)
        outs.append(y * scale if scale != 1.0 else y)
    return jnp.concatenate(outs, axis=1)


def _proj_kernel(x_ref, xp_ref, xn_ref, scl_ref, shl_ref, scc_ref, shc_ref, w_ref, b_ref, cw_ref, cb_ref,
                 cos_ref, sin_ref, o_ref, os_ref, lhs_ref, acc_ref, *, tm, tpb, ctx_len, tb):
    i = pl.program_id(0)
    j = pl.program_id(1)
    p0 = (i % tpb) * tm

    @pl.when(j == 0)
    def _():
        _build_lhs(lhs_ref, x_ref, xp_ref, xn_ref, scl_ref[0], shl_ref[0], scc_ref[...], shc_ref[...],
                   p0, tm, ctx_len)

    def plain_then(epilogue, out_ref=o_ref):
        for s0, n in _sub_blocks(tm):
            a = _mm(lhs_ref[HALO + s0:HALO + s0 + n, :], w_ref[...]) + b_ref[...]
            out_ref[s0:s0 + n, :] = epilogue(a, s0, n).astype(out_ref.dtype)

    def conv_then(epilogue):
        left = LRU_CONV // 2
        for s0, n in _sub_blocks(tm):
            a0 = _acc_row(s0, tm)
            acc_ref[a0 - HALO:a0 + n + HALO, :] = _mm(lhs_ref[s0:s0 + n + 2 * HALO, :], w_ref[...]) + b_ref[...]
        for s0, n in _sub_blocks(tm):
            conv = _dwconv(acc_ref, cw_ref, left, _acc_row(s0, tm), n) + cb_ref[...]
            o_ref[s0:s0 + n, :] = epilogue(conv).astype(o_ref.dtype)
        for r0 in _segment_end_windows(tm, ctx_len):
            fixed = _dwconv(acc_ref, cw_ref, left, _acc_row(r0, tm), HALO, p0 + r0, ctx_len, tb) + cb_ref[...]
            o_ref[r0:r0 + HALO, :] = epilogue(fixed).astype(o_ref.dtype)

    def rope(a, s0, n):
        reps = a.shape[1] // cos_ref.shape[1]
        cos = jnp.concatenate([cos_ref[s0:s0 + n, :]] * reps, axis=1)
        sin = jnp.concatenate([sin_ref[s0:s0 + n, :]] * reps, axis=1)
        return a * cos + _swap_half_heads(a) * sin

    @pl.when(j == T_GLA_QK)
    def _():
        plain_then(rope)

    @pl.when(functools.reduce(jnp.logical_or, [j == t for t in PLAIN_TILES]))
    def _():
        plain_then(lambda a, s0, n: a)

    @pl.when(j == T_SMALL)
    def _():
        plain_then(lambda a, s0, n: a, os_ref)

    @pl.when(j == T_LRU_X)
    def _():
        conv_then(lambda v: v)

    @pl.when(j == T_GDN_Q)
    def _():
        conv_then(lambda v: _l2_heads(_silu(v), GDN_DK, GDN_DK ** -0.5))

    @pl.when(j == T_GDN_K)
    def _():
        conv_then(lambda v: _l2_heads(_silu(v), GDN_DK, 1.0))

    @pl.when(j == T_GDN_V)
    def _():
        conv_then(_silu)

    @pl.when((j >= T_MERGE) & (j < T_SMALL))
    def _():
        plain_then(lambda a, s0, n: _sigmoid(a))


def _proj_call(xa, scl, shl, scc, shc, w, b, cw, cb, cos, sin, *, nb, tb, ctx_len, tm):
    r, d = xa.shape
    n = w.shape[1]
    tpb = tb // tm
    kern = functools.partial(_proj_kernel, tm=tm, tpb=tpb, ctx_len=ctx_len, tb=tb)
    return pl.pallas_call(
        kern,
        out_shape=(jax.ShapeDtypeStruct((r, T_SMALL * TN), BF16), jax.ShapeDtypeStruct((r, TN), F32)),
        grid=(r // tm, n // TN),
        in_specs=_halo_specs(tm, d, r) + [
            pl.BlockSpec((1, 1, d), lambda i, j: (i // tpb, 0, 0)),
            pl.BlockSpec((1, 1, d), lambda i, j: (i // tpb, 0, 0)),
            pl.BlockSpec((1, d), lambda i, j: (0, 0)),
            pl.BlockSpec((1, d), lambda i, j: (0, 0)),
            pl.BlockSpec((d, TN), lambda i, j: (0, j)),
            pl.BlockSpec((1, TN), lambda i, j: (0, j)),
            pl.BlockSpec((LRU_CONV, TN), lambda i, j: (0, j)),
            pl.BlockSpec((1, TN), lambda i, j: (0, j)),
            pl.BlockSpec((tm, 128), lambda i, j: (i % tpb, 0)),
            pl.BlockSpec((tm, 128), lambda i, j: (i % tpb, 0)),
        ],
        out_specs=(pl.BlockSpec((tm, TN), lambda i, j: (i, jnp.minimum(j, T_SMALL - 1))),
                   pl.BlockSpec((tm, TN), lambda i, j: (i, 0))),
        scratch_shapes=[pltpu.VMEM((tm + 2 * HALO, d), BF16),
                        pltpu.VMEM((tm + 2 * HALO * len(_sub_blocks(tm)), TN), F32)],
        compiler_params=_cparams(("arbitrary", "arbitrary")),
        name="proj",
    )(xa, xa, xa, scl, shl, scc, shc, w, b, cw, cb, cos, sin)


def _bwd_index(i, n_ctx, n_tot):
    return jnp.where(i < n_ctx, n_ctx - 1 - i, n_tot + n_ctx - 1 - i)


def _tri_masks(c):
    row = lax.broadcasted_iota(jnp.int32, (c, c), 0)
    col = lax.broadcasted_iota(jnp.int32, (c, c), 1)
    incl = (row >= col, row <= col)
    strict = (row > col, row < col)
    return row, col, incl, strict, tuple(jnp.where(m, 1.0, 0.0).astype(BF16) for m in incl)


def _gla_kernel(qkf_ref, vf_ref, gf_ref, qkb_ref, vb_ref, gb_ref, wg_ref, bg_ref, of_ref, ob_ref, st_ref, *, nb):
    c = CHUNK
    hk = GLA_HEADS * GLA_DK

    @pl.when(pl.program_id(0) == 0)
    def _():
        st_ref[...] = jnp.zeros_like(st_ref)

    _, _, incl, _, tri_bf = _tri_masks(c)
    refs = ((qkf_ref, vf_ref, gf_ref, of_ref), (qkb_ref, vb_ref, gb_ref, ob_ref))
    groups = [(b, d) for b in range(nb) for d in range(2)]
    z = [_mm3(refs[d][2][b, :, d * GLA_GATE_RANK:(d + 1) * GLA_GATE_RANK], wg_ref[d]) + bg_ref[d] for b, d in groups]
    la = [(jnp.minimum(v, 0.0) - jnp.log1p(jnp.exp(-jnp.abs(v)))) * (1.0 / GLA_GATE_NORM) for v in z]
    cum = [_cumdot(tri_bf[d], v) for (b, d), v in zip(groups, la)]
    last = [v[c - 1:c, :] if d == 0 else v[0:1, :] for (b, d), v in zip(groups, cum)]
    mid = [v[c // 2 - 1:c // 2, :] if d == 0 else v[c // 2:c // 2 + 1, :] for (b, d), v in zip(groups, cum)]
    q = [refs[d][0][b, :, :hk].astype(F32) for b, d in groups]
    k = [refs[d][0][b, :, hk:].astype(F32) for b, d in groups]
    vv = [refs[d][1][b].astype(BF16) for b, d in groups]
    n = range(len(groups))
    qd = [(q[g] * jnp.exp(cum[g] - mid[g])).astype(BF16) for g in n]
    km = [(k[g] * jnp.exp(mid[g] - cum[g])).astype(BF16) for g in n]
    kdec = [(k[g] * jnp.exp(last[g] - cum[g])).astype(BF16) for g in n]
    qs = [(q[g] * jnp.exp(cum[g])).astype(BF16) for g in n]
    dec = [jnp.exp(last[g]) for g in n]
    chains = [(g, h) for g in n for h in range(GLA_HEADS)]
    ksl = lambda h: slice(h * GLA_DK, (h + 1) * GLA_DK)
    vsl = lambda h: slice(h * GLA_DV, (h + 1) * GLA_DV)
    st = [st_ref[groups[g][0], groups[g][1], h] for g, h in chains]
    sc = [jnp.where(incl[groups[g][1]], _mm_nt(qd[g][:, ksl(h)], km[g][:, ksl(h)]), 0.0).astype(BF16)
          for g, h in chains]
    o_in = [_mm_nt(qs[g][:, ksl(h)], s.astype(BF16)) for (g, h), s in zip(chains, st)]
    o = [_mm(s, vv[g][:, vsl(h)]) + oi for (g, h), s, oi in zip(chains, sc, o_in)]
    ut = [_mm_tn(vv[g][:, vsl(h)], kdec[g][:, ksl(h)]) for g, h in chains]
    for (g, h), s, u in zip(chains, st, ut):
        st_ref[groups[g][0], groups[g][1], h] = s * dec[g][:, ksl(h)] + u
    for g, (b, d) in enumerate(groups):
        refs[d][3][b] = jnp.concatenate(o[g * GLA_HEADS:(g + 1) * GLA_HEADS], axis=1).astype(refs[d][3].dtype)


def _chunk_specs(nb, width, col, rowf):
    return pl.BlockSpec((nb, CHUNK, width), lambda i: (0, rowf(i), col))


def _gla_call(p3, ps3, wg, bg, *, nb, tb, ctx_len):
    nct, ncc = tb // CHUNK, ctx_len // CHUNK
    fwd = lambda i: i
    bwd = lambda i: _bwd_index(i, ncc, nct)
    width = GLA_HEADS * GLA_DV
    hk = GLA_HEADS * GLA_DK

    def specs(rowf):
        return [_chunk_specs(nb, TN, T_GLA_QK, rowf), _chunk_specs(nb, TN, T_GLA_V, rowf),
                _chunk_specs(nb, 128, 0, rowf)]
    out = jax.ShapeDtypeStruct((nb, tb, width), BF16)
    return pl.pallas_call(
        functools.partial(_gla_kernel, nb=nb),
        out_shape=(out, out),
        grid=(nct,),
        in_specs=specs(fwd) + specs(bwd) + [
            pl.BlockSpec((2, GLA_GATE_RANK, hk), lambda i: (0, 0, 0)),
            pl.BlockSpec((2, 1, hk), lambda i: (0, 0, 0))],
        out_specs=(_chunk_specs(nb, width, 0, fwd), _chunk_specs(nb, width, 0, bwd)),
        scratch_shapes=[pltpu.VMEM((nb, 2, GLA_HEADS, GLA_DV, GLA_DK), F32)],
        compiler_params=_cparams(("arbitrary",)),
        name="gla",
    )(p3, p3, ps3, p3, p3, ps3, wg, bg)


def _lru_kernel(xf_ref, xb_ref, w_ref, gb_ref, lam_ref, of_ref, ob_ref, a_s, b_s, o_s, h_s, *, nb, tl):
    @pl.when(pl.program_id(0) == 0)
    def _():
        h_s[...] = jnp.zeros_like(h_s)

    chains = [(b, d) for b in range(nb) for d in range(2)]
    for n, (b, d) in enumerate(chains):
        xb = (xf_ref, xb_ref)[d][b]
        x = xb.astype(F32)
        gt = _mm(xb, w_ref[d]) + gb_ref[d]
        r = _sigmoid(gt[:, :LRU_WIDTH])
        ig = _sigmoid(gt[:, LRU_WIDTH:])
        log_a = (-LRU_C) * r * _softplus(-lam_ref[d])
        a = jnp.exp(log_a)
        a_s[n] = a
        b_s[n] = jnp.sqrt(1.0 - a * a) * ig * x

    def body(s, hs):
        out = []
        for n, (b, d) in enumerate(chains):
            rr = s if d == 0 else tl - 1 - s
            h = a_s[n, pl.ds(rr, 1), :] * hs[n] + b_s[n, pl.ds(rr, 1), :]
            o_s[n, pl.ds(rr, 1), :] = h
            out.append(h)
        return tuple(out)
    hs = lax.fori_loop(0, tl, body, tuple(h_s[n] for n in range(len(chains))), unroll=8)
    for n, (b, d) in enumerate(chains):
        h_s[n] = hs[n]
        (of_ref, ob_ref)[d][b] = o_s[n].astype(of_ref.dtype)


def _lru_call(p3, wbd, gb, lam, *, nb, tb, ctx_len, tl):
    ntl, nctl = tb // tl, ctx_len // tl
    bwd = lambda i: _bwd_index(i, nctl, ntl)
    out = jax.ShapeDtypeStruct((nb, tb, LRU_WIDTH), BF16)
    const = lambda a: pl.BlockSpec(a.shape, lambda i: (0,) * a.ndim)
    return pl.pallas_call(
        functools.partial(_lru_kernel, nb=nb, tl=tl),
        out_shape=(out, out),
        grid=(ntl,),
        in_specs=[pl.BlockSpec((nb, tl, TN), lambda i: (0, i, T_LRU_X)),
                  pl.BlockSpec((nb, tl, TN), lambda i: (0, bwd(i), T_LRU_X)),
                  const(wbd), const(gb), const(lam)],
        out_specs=(pl.BlockSpec((nb, tl, LRU_WIDTH), lambda i: (0, i, 0)),
                   pl.BlockSpec((nb, tl, LRU_WIDTH), lambda i: (0, bwd(i), 0))),
        scratch_shapes=[pltpu.VMEM((2 * nb, tl, LRU_WIDTH), F32)] * 3 + [pltpu.VMEM((2 * nb, 1, LRU_WIDTH), F32)],
        compiler_params=_cparams(("arbitrary",)),
        name="lru",
    )(p3, p3, wbd, gb, lam)


def _mmb(a, b):
    return _mm(a.astype(BF16), b.astype(BF16))


def _unit_tri_inverses(a_list, row, col):
    eye = jnp.where(row == col, 1.0, 0.0)
    blk = lambda s: (row // s) == (col // s)
    in8 = blk(8)
    a0 = [jnp.where(in8, a, 0.0) for a in a_list]
    p2 = [_mmb(x, x) for x in a0]
    ia = [eye - x for x in a0]
    t = [x + _mmb(x, p) for x, p in zip(ia, p2)]
    p4 = [_mmb(p, p) for p in p2]
    t = [x + _mmb(x, p) for x, p in zip(t, p4)]
    s = 8
    while s < CHUNK:
        sel = blk(2 * s) & jnp.logical_not(blk(s))
        y = [_mmb(jnp.where(sel, a, 0.0), x) for a, x in zip(a_list, t)]
        t = [x - _mmb(x, v) for x, v in zip(t, y)]
        s *= 2
    return t


def _gdn_kernel(qkvf_ref, gf_ref, qkvb_ref, gb_ref, alog_ref, dtb_ref, of_ref, ob_ref, s_ref, *, nb):
    c = CHUNK
    hk = GDN_HEADS * GDN_DK

    @pl.when(pl.program_id(0) == 0)
    def _():
        s_ref[...] = jnp.zeros_like(s_ref)

    row, col, incl, strict, tri_bf = _tri_masks(c)
    refs = ((qkvf_ref, gf_ref, of_ref), (qkvb_ref, gb_ref, ob_ref))
    neg_a = -jnp.exp(alog_ref[...])
    groups = [(b, d) for b in range(nb) for d in range(2)]
    gblk = [refs[d][1][b] for b, d in groups]
    g_all = [neg_a * _softplus(v + dtb_ref[...]) for v in gblk]
    beta_all = [_sigmoid(v) for v in gblk]
    gc_all = [_cumdot(tri_bf[d], v) for (b, d), v in zip(groups, g_all)]
    gc_t = [v.T for v in gc_all]

    chains = [(g, h) for g in range(len(groups)) for h in range(GDN_HEADS)]
    dirn = lambda g: groups[g][1]
    lane_a = lambda g, h: SM_A + dirn(g) * GDN_HEADS + h
    lane_b = lambda g, h: SM_B + dirn(g) * GDN_HEADS + h
    gcol = [gc_all[g][:, lane_a(g, h):lane_a(g, h) + 1] for g, h in chains]
    grow = [gc_t[g][lane_a(g, h):lane_a(g, h) + 1, :] for g, h in chains]
    beta = [beta_all[g][:, lane_b(g, h):lane_b(g, h) + 1] for g, h in chains]
    glast = [v[c - 1:c, :] if dirn(g) == 0 else v[0:1, :] for (g, h), v in zip(chains, gcol)]
    decay = [jnp.exp(jnp.where(incl[dirn(g)], gc - gr, -jnp.inf)) for (g, h), gc, gr in zip(chains, gcol, grow)]

    def head(g, h, part):
        b, d = groups[g]
        return refs[d][0][b, :, part * hk + h * GDN_DK:part * hk + (h + 1) * GDN_DK]
    k_bf = [head(g, h, 1) for g, h in chains]
    qh = [head(g, h, 0).astype(F32) for g, h in chains]
    kh = [k.astype(F32) for k in k_bf]
    vh = [head(g, h, 2).astype(F32) for g, h in chains]
    kb = [k * bt for k, bt in zip(kh, beta)]
    kq = [_mm_nt(jnp.concatenate([x.astype(BF16), y.astype(BF16)], axis=0), k) for x, y, k in zip(kb, qh, k_bf)]
    a = [jnp.where(strict[dirn(g)], x[:c] * dc, 0.0) for (g, h), x, dc in zip(chains, kq, decay)]
    attn = [jnp.where(incl[dirn(g)], x[c:] * dc, 0.0).astype(BF16) for (g, h), x, dc in zip(chains, kq, decay)]
    t = _unit_tri_inverses(a, row, col)
    egc = [jnp.exp(v) for v in gcol]
    sol = [_mmb(x, jnp.concatenate([v * bt, k * e], axis=1)) for x, v, bt, k, e in zip(t, vh, beta, kb, egc)]
    kdec = [(k * jnp.exp(gl - gc)).astype(BF16) for k, gl, gc in zip(kh, glast, gcol)]
    qdec = [(x * e).astype(BF16) for x, e in zip(qh, egc)]
    s = [s_ref[groups[g][0], groups[g][1], h] for g, h in chains]
    s_bf = [v.astype(BF16) for v in s]
    ws = [_mm(jnp.concatenate([x[:, GDN_DV:].astype(BF16), qd], axis=0), sb) for x, qd, sb in zip(sol, qdec, s_bf)]
    v_new = [x[:, :GDN_DV] - y[:c] for x, y in zip(sol, ws)]
    vn_bf = [v.astype(BF16) for v in v_new]
    o = [y[c:] + _mm(at, vn) for y, at, vn in zip(ws, attn, vn_bf)]
    for (g, h), sv, gl, kd, vn in zip(chains, s, glast, kdec, vn_bf):
        s_ref[groups[g][0], groups[g][1], h] = sv * jnp.exp(gl) + _mm_tn(kd, vn)
    for g, (b, d) in enumerate(groups):
        refs[d][2][b] = jnp.concatenate(o[g * GDN_HEADS:(g + 1) * GDN_HEADS], axis=1).astype(refs[d][2].dtype)


def _gdn_call(p3, ps3, alog, dtb, *, nb, tb, ctx_len):
    nct, ncc = tb // CHUNK, ctx_len // CHUNK
    fwd = lambda i: i
    bwd = lambda i: _bwd_index(i, ncc, nct)
    width = GDN_HEADS * GDN_DV
    qkv_w = 3 * TN

    def specs(rowf):
        return [_chunk_specs(nb, qkv_w, T_GDN_Q * TN // qkv_w, rowf), _chunk_specs(nb, 128, 0, rowf)]
    out = jax.ShapeDtypeStruct((nb, tb, width), BF16)
    return pl.pallas_call(
        functools.partial(_gdn_kernel, nb=nb),
        out_shape=(out, out),
        grid=(nct,),
        in_specs=specs(fwd) + specs(bwd) + [pl.BlockSpec((1, 128), lambda i: (0, 0)),
                                            pl.BlockSpec((1, 128), lambda i: (0, 0))],
        out_specs=(_chunk_specs(nb, width, 0, fwd), _chunk_specs(nb, width, 0, bwd)),
        scratch_shapes=[pltpu.VMEM((nb, 2, GDN_HEADS, GDN_DK, GDN_DV), F32)],
        compiler_params=_cparams(("arbitrary",)),
        name="gdn",
    )(p3, ps3, p3, ps3, alog, dtb)


def _na_kernel(q_ref, k0_ref, k1_ref, k2_ref, v0_ref, v1_ref, v2_ref, kc_ref, vc_ref, bias_ref, o_ref,
               kbuf, vbuf, vcbuf, *, rows):
    jj = pl.program_id(1)
    tq = NA_ROWS * GRID_W
    hd = NA_HD

    @pl.when(jj == 0)
    def _():
        outs = []
        for h in range(NA_HEADS):
            sl = slice(h * hd, (h + 1) * hd)
            s = _mm_nt(q_ref[:, sl], k1_ref[:, sl])
            m = jnp.max(s, axis=-1, keepdims=True)
            e = jnp.exp(s - m)
            l = jnp.sum(e, axis=-1, keepdims=True)
            outs.append(_mm(e.astype(BF16), v1_ref[:, sl]) / l)
        o_ref[...] = jnp.concatenate(outs, axis=1).astype(o_ref.dtype)

    @pl.when(jj > 0)
    def _():
        jp = jj - 1
        pw = 2 * hd
        pairs = range(NA_HEADS // 2)
        ones = jnp.ones((tq, pw), BF16)
        for n, (kr, vr) in enumerate(((k0_ref, v0_ref), (k1_ref, v1_ref), (k2_ref, v2_ref))):
            kbuf[n * tq:(n + 1) * tq, :] = kr[...]
            for pr in pairs:
                vbuf[n * tq:(n + 1) * tq, 2 * pr * pw:(2 * pr + 1) * pw] = vr[:, pr * pw:(pr + 1) * pw]
                vbuf[n * tq:(n + 1) * tq, (2 * pr + 1) * pw:(2 * pr + 2) * pw] = ones
        for pr in pairs:
            vcbuf[:, 2 * pr * pw:(2 * pr + 1) * pw] = vc_ref[:, pr * pw:(pr + 1) * pw]
            vcbuf[:, (2 * pr + 1) * pw:(2 * pr + 2) * pw] = ones[:vcbuf.shape[0]]
        first = lax.broadcasted_iota(jnp.int32, (1, pw), 1) < hd
        for a in range(NA_ROWS):
            r = NA_ROWS * jp + a
            rs = jnp.clip(r - NA_WIN_R // 2, 0, rows - NA_WIN_R)
            start = pl.multiple_of((rs - NA_ROWS * (jp - 1)) * GRID_W, GRID_W)
            e_idx = r - rs
            win = pl.ds(start, NA_WIN_R * GRID_W)
            qp = [q_ref[a * GRID_W:(a + 1) * GRID_W, pr * pw:(pr + 1) * pw] for pr in pairs]
            zero = jnp.zeros_like(qp[0])
            q2 = [jnp.concatenate([jnp.where(first, x, zero), jnp.where(first, zero, x)], axis=0) for x in qp]
            s_loc = [_mm_nt(q2[pr], kbuf[win, pr * pw:(pr + 1) * pw]) + bias_ref[e_idx, pr] for pr in pairs]
            s_ctx = [_mm_nt(q2[pr], kc_ref[:, pr * pw:(pr + 1) * pw]) for pr in pairs]
            nk = s_ctx[0].shape[1]
            m = []
            for pr in pairs:
                mx = s_ctx[pr]
                for k0 in range(0, s_loc[pr].shape[1], nk):
                    mx = jnp.maximum(mx, s_loc[pr][:, k0:k0 + nk])
                m.append(jnp.max(mx, axis=-1, keepdims=True))
            p_loc = [jnp.exp(s_loc[pr] - m[pr]).astype(BF16) for pr in pairs]
            p_ctx = [jnp.exp(s_ctx[pr] - m[pr]).astype(BF16) for pr in pairs]
            oe = [_mm(p_loc[pr], vbuf[win, 2 * pr * pw:(2 * pr + 2) * pw])
                  + _mm(p_ctx[pr], vcbuf[:, 2 * pr * pw:(2 * pr + 2) * pw]) for pr in pairs]
            on = [x[:, :pw] / x[:, pw:] for x in oe]
            o = [jnp.where(first, x[:GRID_W], x[GRID_W:]) for x in on]
            o_ref[a * GRID_W:(a + 1) * GRID_W, :] = jnp.concatenate(o, axis=1).astype(o_ref.dtype)


def _na_call(p, bias, *, nb, tb, ctx_len):
    r = p.shape[0]
    tq = NA_ROWS * GRID_W
    assert ctx_len == tq, "context block must be exactly one query block"
    nblk = tb // tq
    nlb = nblk - 1
    rows = (tb - ctx_len) // GRID_W
    assert rows >= NA_WIN_R and rows % NA_ROWS == 0
    width = NA_HEADS * NA_HD
    cur = lambda b, j: b * nblk + j
    prv = lambda b, j: b * nblk + 1 + jnp.clip(j - 2, 0, nlb - 1)
    nxt = lambda b, j: b * nblk + 1 + jnp.clip(j, 0, nlb - 1)
    blk = lambda rowf, t: pl.BlockSpec((tq, TN), lambda b, j: (rowf(b, j), t))
    return pl.pallas_call(
        functools.partial(_na_kernel, rows=rows),
        out_shape=jax.ShapeDtypeStruct((r, width), BF16),
        grid=(nb, nblk),
        in_specs=[blk(cur, T_NA_Q),
                  blk(prv, T_NA_K), blk(cur, T_NA_K), blk(nxt, T_NA_K),
                  blk(prv, T_NA_V), blk(cur, T_NA_V), blk(nxt, T_NA_V),
                  pl.BlockSpec((ctx_len, TN), lambda b, j: (b * nblk, T_NA_K)),
                  pl.BlockSpec((ctx_len, TN), lambda b, j: (b * nblk, T_NA_V)),
                  pl.BlockSpec(bias.shape, lambda b, j: (0, 0, 0, 0), pipeline_mode=pl.Buffered(1))],
        out_specs=pl.BlockSpec((tq, width), lambda b, j: (cur(b, j), 0)),
        scratch_shapes=[pltpu.VMEM((3 * tq, TN), BF16), pltpu.VMEM((3 * tq, 2 * TN), BF16),
                        pltpu.VMEM((ctx_len, 2 * TN), BF16)],
        compiler_params=_cparams(("arbitrary", "arbitrary")),
        name="na",
    )(p, p, p, p, p, p, p, p, p, bias)


def _na_bias_table(rpb):
    c = np.arange(GRID_W)
    cs = np.clip(c - NA_WIN_C // 2, 0, GRID_W - NA_WIN_C)
    kc = np.arange(GRID_W)
    inwin = (kc[None, :] >= cs[:, None]) & (kc[None, :] < cs[:, None] + NA_WIN_C)
    dc = np.clip(kc[None, :] - c[:, None] + NA_WIN_C - 1, 0, 2 * NA_WIN_C - 2)
    onehot = np.zeros((2 * NA_WIN_C - 1, GRID_W * GRID_W), np.float32)
    onehot[dc.reshape(-1), np.arange(GRID_W * GRID_W)] = 1.0
    byrow = jnp.einsum('hrd,dm->hrm', rpb, onehot, precision=lax.Precision.HIGHEST)
    byrow = jnp.where(inwin[None, None], byrow.reshape(NA_HEADS, 2 * NA_WIN_R - 1, GRID_W, GRID_W), NEG_BIG)
    tabs = [jnp.transpose(byrow[:, NA_WIN_R - 1 - e:2 * NA_WIN_R - 1 - e], (0, 2, 1, 3))
            .reshape(NA_HEADS, GRID_W, NA_WIN_R * GRID_W) for e in range(NA_WIN_R)]
    return jnp.stack(tabs).reshape(NA_WIN_R, NA_HEADS // 2, 2 * GRID_W, NA_WIN_R * GRID_W)


def _head_rms(o, g, width):
    outs = []
    for h in range(o.shape[1] // width):
        xs = o[:, h * width:(h + 1) * width]
        ms = jnp.mean(xs * xs, axis=-1, keepdims=True)
        outs.append(xs * lax.rsqrt(ms + NORM_EPS) * g)
    return jnp.concatenate(outs, axis=1)


def _merge_kernel(gof_ref, gob_ref, gg_ref, lhf_ref, lhb_ref, ly_ref, dof_ref, dob_ref, dz_ref, na_ref,
                  m0_ref, m1_ref, m2_ref, m3_ref, x_ref, g1l_ref, g1c_ref, wbr_ref, wout_ref, bout_ref,
                  gng_ref, gnd_ref, lng_ref, lnb_ref, o_ref, *, tm, tpb, ctx_len, alpha):
    p0 = (pl.program_id(0) % tpb) * tm
    f = lambda ref: ref[...].astype(F32)
    y_a = _head_rms(f(gof_ref) + f(gob_ref), gng_ref[...], GLA_DV) * _silu(f(gg_ref))
    y_b = (f(lhf_ref) + f(lhb_ref)) * _gelu_tanh(f(ly_ref))
    y_c = _head_rms(f(dof_ref) + f(dob_ref), gnd_ref[...], GDN_DV) * _silu(f(dz_ref))
    y_d = na_ref[...]
    m = None
    for n, (y, g_ref) in enumerate(((y_a, m0_ref), (y_b, m1_ref), (y_c, m2_ref), (y_d, m3_ref))):
        term = g_ref[...].astype(F32) * _mm(y.astype(BF16), wbr_ref[n])
        m = term if m is None else m + term
    out = _mm(m.astype(BF16), wout_ref[...]) + bout_ref[...]
    is_ctx = _row_pos(tm, p0) < ctx_len
    g1 = jnp.where(is_ctx, g1c_ref[...], g1l_ref[0])
    o_ref[...] = _layer_norm(alpha * x_ref[...] + g1 * out, lng_ref[...], lnb_ref[...])


def _merge_call(p, gof, gob, lhf, lhb, dof, dob, na, xa, g1l, g1c, wbr, wout, bout, gng, gnd, lng, lnb,
                *, tb, ctx_len, tm, alpha):
    r, d = xa.shape
    tpb = tb // tm
    bw = BRANCH_W
    row = lambda w: pl.BlockSpec((tm, w), lambda i: (i, 0))
    ptile = lambda t, w: pl.BlockSpec((tm, w), lambda i: (i, t * TN // w))
    const = lambda a: pl.BlockSpec(a.shape, lambda i: (0,) * a.ndim)
    kern = functools.partial(_merge_kernel, tm=tm, tpb=tpb, ctx_len=ctx_len, alpha=alpha)
    return pl.pallas_call(
        kern,
        out_shape=jax.ShapeDtypeStruct((r, d), F32),
        grid=(r // tm,),
        in_specs=[row(bw), row(bw), ptile(T_GLA_G, bw),
                  row(bw), row(bw),
                  ptile(T_LRU_Y, bw),
                  row(bw), row(bw), ptile(T_GDN_Z, bw),
                  row(bw),
                  pl.BlockSpec((tm, d), lambda i: (i, T_MERGE * TN // d + 0)),
                  pl.BlockSpec((tm, d), lambda i: (i, T_MERGE * TN // d + 1)),
                  pl.BlockSpec((tm, d), lambda i: (i, T_MERGE * TN // d + 2)),
                  pl.BlockSpec((tm, d), lambda i: (i, T_MERGE * TN // d + 3)),
                  row(d),
                  pl.BlockSpec((1, 1, d), lambda i: (i // tpb, 0, 0)),
                  const(g1c), const(wbr), const(wout), const(bout), const(gng), const(gnd), const(lng), const(lnb)],
        out_specs=row(d),
        compiler_params=_cparams(("arbitrary",)),
        name="merge",
    )(gof, gob, p, lhf, lhb, p, dof, dob, p, na, p, p, p, p, xa, g1l, g1c, wbr, wout, bout, gng, gnd, lng, lnb)


def _ffn_kernel(x_ref, xp_ref, xn_ref, scl_ref, shl_ref, gl_ref, scc_ref, shc_ref, gc_ref, wa_ref, wb_ref,
                ba_ref, bb_ref, cw_ref, cb_ref, wd_ref, bd_ref, lng_ref, lnb_ref, o_ref,
                lhs_ref, a_ref, h_ref, *, tm, tpb, ctx_len, tb, nc, tc, alpha):
    i = pl.program_id(0)
    c = pl.program_id(1)
    p0 = (i % tpb) * tm

    @pl.when(c == 0)
    def _():
        _build_lhs(lhs_ref, x_ref, xp_ref, xn_ref, scl_ref[0], shl_ref[0], scc_ref[...], shc_ref[...],
                   p0, tm, ctx_len)

    left = FFN_CONV // 2
    subs = _sub_blocks(tm)
    for s0, n in subs:
        a0 = _acc_row(s0, tm)
        a_ref[a0 - HALO:a0 + n + HALO, :] = _mm(lhs_ref[s0:s0 + n + 2 * HALO, :], wa_ref[...]) + ba_ref[...]
    gates = [_mm(lhs_ref[HALO + s0:HALO + s0 + n, :], wb_ref[...]) + bb_ref[...] for s0, n in subs]
    hidden = [(_silu(_dwconv(a_ref, cw_ref, left, _acc_row(s0, tm), n) + cb_ref[...]) * g).astype(BF16)
              for (s0, n), g in zip(subs, gates)]

    def gate_rows(r0):
        n = subs[0][1]
        return gates[r0 // n][r0 % n:r0 % n + HALO, :]
    fixes = [(r0, (_silu(_dwconv(a_ref, cw_ref, left, _acc_row(r0, tm), HALO, p0 + r0, ctx_len, tb) + cb_ref[...])
                   * gate_rows(r0)).astype(BF16)) for r0 in _segment_end_windows(tm, ctx_len)]
    for cc in range(nc):
        @pl.when(c == cc)
        def _(cc=cc):
            for (s0, n), h in zip(subs, hidden):
                h_ref[s0:s0 + n, cc * tc:(cc + 1) * tc] = h
            for r0, fixed in fixes:
                h_ref[r0:r0 + HALO, cc * tc:(cc + 1) * tc] = fixed

    @pl.when(c == nc - 1)
    def _():
        f = _mm(h_ref[...], wd_ref[...]) + bd_ref[...]
        is_ctx = _row_pos(tm, p0) < ctx_len
        g2 = jnp.where(is_ctx, gc_ref[...], gl_ref[0])
        o_ref[...] = _layer_norm(alpha * x_ref[...] + g2 * f, lng_ref[...], lnb_ref[...])


def _ffn_call(xa, scl, shl, gl, scc, shc, gc, w_up, b_up, cw, cb, w_down, b_down, lng, lnb,
              *, tb, ctx_len, tm, tc, alpha):
    r, d = xa.shape
    dff = w_down.shape[0]
    nc = dff // tc
    tpb = tb // tm
    lat = pl.BlockSpec((1, 1, d), lambda i, c: (i // tpb, 0, 0))
    vec = pl.BlockSpec((1, d), lambda i, c: (0, 0))
    kern = functools.partial(_ffn_kernel, tm=tm, tpb=tpb, ctx_len=ctx_len, tb=tb, nc=nc, tc=tc, alpha=alpha)
    return pl.pallas_call(
        kern,
        out_shape=jax.ShapeDtypeStruct((r, d), F32),
        grid=(r // tm, nc),
        in_specs=_halo_specs(tm, d, r) + [
            lat, lat, lat, vec, vec, vec,
            pl.BlockSpec((d, tc), lambda i, c: (0, c)),
            pl.BlockSpec((d, tc), lambda i, c: (0, nc + c)),
            pl.BlockSpec((1, tc), lambda i, c: (0, c)),
            pl.BlockSpec((1, tc), lambda i, c: (0, nc + c)),
            pl.BlockSpec((FFN_CONV, tc), lambda i, c: (0, c)),
            pl.BlockSpec((1, tc), lambda i, c: (0, c)),
            pl.BlockSpec((dff, d), lambda i, c: (0, 0), pipeline_mode=pl.Buffered(1)),
            vec, vec, vec],
        out_specs=pl.BlockSpec((tm, d), lambda i, c: (i, 0)),
        scratch_shapes=[pltpu.VMEM((tm + 2 * HALO, d), BF16),
                        pltpu.VMEM((tm + 2 * HALO * len(_sub_blocks(tm)), tc), F32),
                        pltpu.VMEM((tm, dff), BF16)],
        compiler_params=_cparams(("arbitrary", "arbitrary")),
        name="ffn",
    )(xa, xa, xa, scl, shl, gl, scc, shc, gc, w_up, w_up, b_up, b_up, cw, cb, w_down, b_down, lng, lnb)


def _pad_cols(a, width):
    return jnp.pad(a, ((0, 0), (0, width - a.shape[1])))


def _in_proj_params(w_in, b_in, lru_conv_w, lru_conv_b, gdn_conv_w):
    sizes = (GLA_HEADS * GLA_DK, GLA_HEADS * GLA_DK, GLA_HEADS * GLA_DV, GLA_HEADS * GLA_DV,
             GLA_GATE_RANK, GLA_GATE_RANK, LRU_WIDTH, LRU_WIDTH,
             GDN_HEADS * GDN_DK, GDN_HEADS * GDN_DK, GDN_HEADS * GDN_DV, GDN_HEADS * GDN_DV,
             2 * GDN_HEADS, 2 * GDN_HEADS, NA_HEADS * NA_HD, NA_HEADS * NA_HD, NA_HEADS * NA_HD)
    names = ('gla_q', 'gla_k', 'gla_v', 'gla_g', 'gla_fw', 'gla_bw', 'lru_x', 'lru_y', 'gdn_q', 'gdn_k',
             'gdn_v', 'gdn_z', 'gdn_a', 'gdn_b', 'na_q', 'na_k', 'na_v')
    offs = np.concatenate([[0], np.cumsum(sizes)])

    def reorder(m):
        col = {n: m[:, offs[k]:offs[k + 1]] for k, n in enumerate(names)}
        col['merge'] = m[:, offs[-1]:]
        small = _pad_cols(jnp.concatenate([col['gla_fw'], col['gla_bw'], col['gdn_a'], col['gdn_b']], axis=1), TN)
        tiles = [jnp.concatenate([col['gla_q'] * (GLA_DK ** -0.5), col['gla_k']], axis=1),
                 col['gla_v'], col['gla_g'],
                 col['gdn_q'], col['gdn_k'], col['gdn_v'], col['gdn_z'], col['lru_x'], col['lru_y'],
                 col['na_q'] * (NA_HD ** -0.5), col['na_k'], col['na_v'], col['merge'], small]
        return jnp.concatenate(tiles, axis=1)
    w = reorder(w_in.astype(BF16))
    b = reorder(b_in[None, :])
    n = w.shape[1]
    assert n == N_TILES * TN
    cw = jnp.zeros((LRU_CONV, n), F32)
    cw = cw.at[:, T_LRU_X * TN:(T_LRU_X + 1) * TN].set(lru_conv_w)
    cw = cw.at[:, T_GDN_Q * TN:(T_GDN_V + 1) * TN].set(gdn_conv_w)
    cb = jnp.zeros((1, n), F32).at[0, T_LRU_X * TN:(T_LRU_X + 1) * TN].set(lru_conv_b)
    return w, b, cw, cb


def _rope_tables(ctx_len, seq_len):
    quarter = GLA_DK // 4
    t = jnp.arange(seq_len)
    inv = ROPE_BASE ** (-jnp.arange(quarter, dtype=F32) / quarter)
    ang = jnp.concatenate([(t // GRID_W).astype(F32)[:, None] * inv,
                           (t % GRID_W).astype(F32)[:, None] * inv], -1)
    cos = jnp.concatenate([jnp.ones((ctx_len, 2 * quarter), F32), jnp.cos(ang)], axis=0)
    sin = jnp.concatenate([jnp.zeros((ctx_len, 2 * quarter), F32), jnp.sin(ang)], axis=0)
    return jnp.concatenate([cos, cos, cos, cos], axis=1), jnp.concatenate([-sin, sin, -sin, sin], axis=1)


def _lru_gate_matrix(gate_w):
    eye = jnp.eye(LRU_BLOCKS, dtype=F32)
    dense = jnp.einsum('xgnde,nm->xndgme', gate_w, eye)
    return dense.reshape(2, LRU_WIDTH, 2 * LRU_WIDTH).astype(BF16)


def _small_lane_vec(v):
    return jnp.zeros((1, 128), F32).at[0, SM_A:SM_A + 2 * GDN_HEADS].set(v.reshape(-1))


def _row_tile(tb, target):
    best = HALO
    for t in range(HALO, target + 1, HALO):
        if tb % t == 0:
            best = t
    return best


def kernel(x, c, ctx, c_ctx, w_mod, b_mod, w_in, b_in, gla_w_gate, gla_b_gate, gla_norm, lru_conv_w, lru_conv_b,
           lru_gate_w, lru_gate_b, lru_lambda, gdn_conv_w, gdn_a_log, gdn_dt_bias, gdn_norm, na_rpb, w_branch,
           w_out, b_out, ln1_g, ln1_b, ffn_w_up, ffn_b_up, ffn_conv_w, ffn_conv_b, ffn_w_down, ffn_b_down,
           ln2_g, ln2_b):
    nb, seq_len, d = x.shape
    ctx_len = ctx.shape[1]
    depth = w_mod.shape[0]
    tb = ctx_len + seq_len
    alpha = (2 * depth) ** 0.25
    tm = _row_tile(tb, ROW_TILE)
    tl = NA_ROWS * GRID_W
    dims = dict(nb=nb, tb=tb, ctx_len=ctx_len)

    xa = jnp.concatenate([ctx, x], axis=1).reshape(nb * tb, d)
    crows = jnp.zeros((8, d), F32).at[:nb].set(c).at[nb].set(c_ctx)
    mods = _mod_call(crows, w_mod, b_mod)
    cos, sin = _rope_tables(ctx_len, seq_len)

    for l in range(depth):
        lat = [mods[l, :nb, k * d:(k + 1) * d].reshape(nb, 1, d) for k in range(6)]
        cx = [mods[l, nb:nb + 1, k * d:(k + 1) * d] for k in range(6)]
        w, b, cw, cb = _in_proj_params(w_in[l], b_in[l], lru_conv_w[l], lru_conv_b[l], gdn_conv_w[l])
        p, ps = _proj_call(xa, lat[1], lat[0], cx[1], cx[0], w, b, cw, cb, cos, sin, tm=tm, **dims)

        p3, ps3 = p.reshape(nb, tb, -1), ps.reshape(nb, tb, -1)
        flat = lambda a: a.reshape(nb * tb, -1)
        gof, gob = map(flat, _gla_call(p3, ps3, gla_w_gate[l], gla_b_gate[l].reshape(2, 1, -1), **dims))
        lhf, lhb = map(flat, _lru_call(p3, _lru_gate_matrix(lru_gate_w[l]), lru_gate_b[l].reshape(2, 1, -1),
                                       lru_lambda[l].reshape(2, 1, -1), tl=tl, **dims))
        dof, dob = map(flat, _gdn_call(p3, ps3, _small_lane_vec(gdn_a_log[l]), _small_lane_vec(gdn_dt_bias[l]),
                                       **dims))
        na = _na_call(p, _na_bias_table(na_rpb[l]), **dims)

        xa = _merge_call(p, gof, gob, lhf, lhb, dof, dob, na, xa, lat[2], cx[2],
                         w_branch[l].astype(BF16), w_out[l].astype(BF16), b_out[l].reshape(1, d),
                         gla_norm[l].reshape(1, -1),
                         gdn_norm[l].reshape(1, -1), ln1_g[l].reshape(1, d), ln1_b[l].reshape(1, d),
                         tb=tb, ctx_len=ctx_len, tm=tl, alpha=alpha)
        xa = _ffn_call(xa, lat[4], lat[3], lat[5], cx[4], cx[3], cx[5],
                       ffn_w_up[l].astype(BF16), ffn_b_up[l].reshape(1, -1), ffn_conv_w[l],
                       ffn_conv_b[l].reshape(1, -1), ffn_w_down[l].astype(BF16), ffn_b_down[l].reshape(1, d),
                       ln2_g[l].reshape(1, d), ln2_b[l].reshape(1, d),
                       tb=tb, ctx_len=ctx_len, tm=tm, tc=256, alpha=alpha)
    return xa.reshape(nb, tb, d)[:, ctx_len:, :]
```

```python
import functools
import math

import jax
import jax.numpy as jnp
import numpy as np
from jax import lax
from jax.experimental import pallas as pl
from jax.experimental.pallas import tpu as pltpu

F32 = jnp.float32
BF16 = jnp.bfloat16

GRID_W = 64
CHUNK = 64
GLA_HEADS, GLA_DK, GLA_DV = 4, 64, 128
GLA_GATE_RANK = 16
GLA_GATE_NORM = 16.0
ROPE_BASE = 10000.0
LRU_WIDTH, LRU_BLOCKS, LRU_CONV, LRU_C = 512, 8, 4, 8.0
GDN_HEADS, GDN_DK, GDN_DV, GDN_CONV = 4, 128, 128, 4
NA_HEADS, NA_HD, NA_WIN_R, NA_WIN_C = 8, 64, 8, 16
FFN_CONV = 3
LN_EPS = 1e-5
NORM_EPS = 1e-6
BRANCH_W = 512

HALO = 16
ROW_TILE = 1280
MERGE_TILE = 512
ROW_SUB = 4
TN = 512
NA_ROWS = 4
VMEM_LIMIT = 56 * 1024 * 1024
NEG_BIG = -1e30

T_GLA_QK, T_GLA_V = 0, 1
T_GDN_Q, T_GLA_G = 2, 3
T_GDN_K, T_GDN_Z = 4, 5
T_GDN_V, T_LRU_Y = 6, 7
T_LRU_X, T_NA_Q = 8, 9
T_NA_K, T_NA_V = 10, 11
T_MERGE = 12
T_SMALL = 20
N_TILES = 22
SM_FW, SM_BW, SM_A, SM_B = 0, 16, 32, 40


def _cparams(sem):
    return pltpu.CompilerParams(dimension_semantics=sem, vmem_limit_bytes=VMEM_LIMIT)


def _mm(a, b):
    return jnp.dot(a, b, preferred_element_type=F32)


def _mm_nt(a, b):
    return lax.dot_general(a, b, (((1,), (1,)), ((), ())), preferred_element_type=F32)


def _mm_tn(a, b):
    return lax.dot_general(a, b, (((0,), (0,)), ((), ())), preferred_element_type=F32)


def _mmb(a, b):
    return _mm(a.astype(BF16), b.astype(BF16))


def _cumdot(tri_bf, x):
    hi = x.astype(BF16)
    lo = (x - hi.astype(F32)).astype(BF16)
    return _mm(tri_bf, hi) + _mm(tri_bf, lo)


def _sigmoid(x):
    return 0.5 * jnp.tanh(0.5 * x) + 0.5


def _silu(x):
    return x * _sigmoid(x)


def _softplus(x):
    return jnp.maximum(x, 0.0) + jnp.log1p(jnp.exp(-jnp.abs(x)))


def _gelu_tanh(x):
    return x * (0.5 * (1.0 + jnp.tanh(math.sqrt(2.0 / math.pi) * (x + 0.044715 * (x * x * x)))))


def _layer_norm(y, g, b):
    mu = jnp.mean(y, axis=-1, keepdims=True)
    yc = y - mu
    var = jnp.mean(yc * yc, axis=-1, keepdims=True)
    return yc * lax.rsqrt(var + LN_EPS) * g + b


def _row_pos(n, off):
    return lax.broadcasted_iota(jnp.int32, (n, 1), 0) + off


def _mod_kernel(c_ref, w_ref, b_ref, o_ref):
    cv = c_ref[...]
    o_ref[0] = jnp.dot(_silu(cv), w_ref[0], precision=lax.Precision.HIGHEST,
                       preferred_element_type=F32) + b_ref[0]


def _mod_call(crows, w_mod, b_mod):
    depth, d, n = w_mod.shape
    tn = n // 4
    return pl.pallas_call(
        _mod_kernel,
        out_shape=jax.ShapeDtypeStruct((depth, 8, n), F32),
        grid=(depth, n // tn),
        in_specs=[pl.BlockSpec((8, d), lambda l, j: (0, 0)),
                  pl.BlockSpec((1, d, tn), lambda l, j: (l, 0, j)),
                  pl.BlockSpec((1, 1, tn), lambda l, j: (l, 0, j))],
        out_specs=pl.BlockSpec((1, 8, tn), lambda l, j: (l, 0, j)),
        compiler_params=_cparams(("arbitrary", "arbitrary")),
        name="mod",
    )(crows, w_mod, b_mod.reshape(depth, 1, n))


def _build_lhs(lhs_ref, x_ref, xp_ref, xn_ref, scl, shl, scc, shc, p0, tm, ctx_len):
    def mod(xv, p):
        is_ctx = (p >= 0) & (p < ctx_len)
        sc = jnp.where(is_ctx, scc, scl)
        sh = jnp.where(is_ctx, shc, shl)
        return (xv * (1.0 + sc) + sh).astype(BF16)
    lhs_ref[0:HALO, :] = mod(xp_ref[...], _row_pos(HALO, p0 - HALO))
    lhs_ref[HALO:HALO + tm, :] = mod(x_ref[...], _row_pos(tm, p0))
    lhs_ref[HALO + tm:, :] = mod(xn_ref[...], _row_pos(HALO, p0 + tm))


def _dwconv(acc_ref, cw, left, a0, n, pos=None, ctx_len=None, tb=None):
    if pos is not None:
        p = _row_pos(n, pos)
        in_ctx = p < ctx_len
        seg_lo = jnp.where(in_ctx, 0, ctx_len)
        seg_hi = jnp.where(in_ctx, ctx_len, tb)
    out = None
    if pos is None and a0 % 8 == 0 and n % 8 == 0:
        blk = acc_ref[pl.ds(a0 - HALO, n + 2 * HALO), :]
        for k in range(cw.shape[0]):
            d = k - left
            v = blk if d == 0 else pltpu.roll(blk, (-d) % blk.shape[0], 0)
            term = v[HALO:HALO + n, :] * cw[k:k + 1, :]
            out = term if out is None else out + term
        return out
    for k in range(cw.shape[0]):
        d = k - left
        v = acc_ref[pl.ds(a0 + d, n), :]
        if pos is not None:
            v = jnp.where((p + d >= seg_lo) & (p + d < seg_hi), v, 0.0)
        term = v * cw[k:k + 1, :]
        out = term if out is None else out + term
    return out


def _sub_blocks(tm):
    n = tm // ROW_SUB if tm % (ROW_SUB * HALO) == 0 else tm
    return [(s0, n) for s0 in range(0, tm, n)]


def _acc_row(r, tm):
    n = _sub_blocks(tm)[0][1]
    return (r // n) * (n + 2 * HALO) + HALO + r % n


def _segment_end_windows(tm, ctx_len):
    starts = {0, tm - HALO}
    if ctx_len % tm:
        starts |= {ctx_len % tm - HALO, ctx_len % tm}
    return sorted(starts)


def _halo_specs(tm, d, r):
    hb = tm // HALO
    return [pl.BlockSpec((tm, d), lambda i, j: (i, 0)),
            pl.BlockSpec((HALO, d), lambda i, j: (jnp.maximum(i * hb - 1, 0), 0)),
            pl.BlockSpec((HALO, d), lambda i, j: (jnp.minimum((i + 1) * hb, r // HALO - 1), 0))]


def _swap_half_heads(x):
    n = x.shape[1]
    lane = lax.broadcasted_iota(jnp.int32, x.shape, 1)
    return jnp.where((lane % 64) < 32, pltpu.roll(x, n - 32, 1), pltpu.roll(x, 32, 1))


def _l2_heads(x, width, scale):
    outs = []
    for h in range(x.shape[1] // width):
        xs = x[:, h * width:(h + 1) * width]
        ss = jnp.sum(xs * xs, axis=-1, keepdims=True)
        y = xs * lax.rsqrt(ss + NORM_EPS)
        outs.append(y * scale if scale != 1.0 else y)
    return jnp.concatenate(outs, axis=1)


def _proj_kernel(x_ref, xp_ref, xn_ref, scl_ref, shl_ref, scc_ref, shc_ref, w_ref, b_ref, cw_ref, cb_ref,
                 cos_ref, sin_ref, o_ref, os_ref, lhs_ref, acc_ref, *, tm, tpb, ctx_len, tb):
    i = pl.program_id(0)
    j = pl.program_id(1)
    p0 = (i % tpb) * tm

    @pl.when(j == 0)
    def _():
        _build_lhs(lhs_ref, x_ref, xp_ref, xn_ref, scl_ref[0], shl_ref[0], scc_ref[...], shc_ref[...],
                   p0, tm, ctx_len)

    def col(c):
        return slice(c * TN, (c + 1) * TN)

    def plain_then(c, epilogue, out_ref=o_ref):
        for s0, n in _sub_blocks(tm):
            a = _mm(lhs_ref[HALO + s0:HALO + s0 + n, :], w_ref[:, col(c)]) + b_ref[:, col(c)]
            out_ref[s0:s0 + n, col(c)] = epilogue(a, s0, n).astype(out_ref.dtype)

    def conv_then(c, epilogue):
        left = LRU_CONV // 2
        cw = cw_ref[:, col(c)]
        cb = cb_ref[:, col(c)]
        for s0, n in _sub_blocks(tm):
            a0 = _acc_row(s0, tm)
            acc_ref[a0 - HALO:a0 + n + HALO, :] = (_mm(lhs_ref[s0:s0 + n + 2 * HALO, :], w_ref[:, col(c)])
                                                   + b_ref[:, col(c)])
        for s0, n in _sub_blocks(tm):
            conv = _dwconv(acc_ref, cw, left, _acc_row(s0, tm), n) + cb
            o_ref[s0:s0 + n, col(c)] = epilogue(conv).astype(o_ref.dtype)
        for r0 in _segment_end_windows(tm, ctx_len):
            fixed = _dwconv(acc_ref, cw, left, _acc_row(r0, tm), HALO, p0 + r0, ctx_len, tb) + cb
            o_ref[r0:r0 + HALO, col(c)] = epilogue(fixed).astype(o_ref.dtype)

    def rope(a, s0, n):
        reps = a.shape[1] // cos_ref.shape[1]
        cos = jnp.concatenate([cos_ref[s0:s0 + n, :]] * reps, axis=1)
        sin = jnp.concatenate([sin_ref[s0:s0 + n, :]] * reps, axis=1)
        return a * cos + _swap_half_heads(a) * sin

    ident = lambda a, s0, n: a
    conv_epilogues = {T_GDN_Q: lambda v: _l2_heads(_silu(v), GDN_DK, GDN_DK ** -0.5),
                      T_GDN_K: lambda v: _l2_heads(_silu(v), GDN_DK, 1.0),
                      T_GDN_V: _silu,
                      T_LRU_X: lambda v: v}
    for step in range(T_MERGE // 2):
        @pl.when(j == step)
        def _(step=step):
            for c, t in enumerate((2 * step, 2 * step + 1)):
                if t in conv_epilogues:
                    conv_then(c, conv_epilogues[t])
                else:
                    plain_then(c, rope if t == T_GLA_QK else ident)

    @pl.when((j >= T_MERGE // 2) & (j < T_SMALL // 2))
    def _():
        for c in range(2):
            plain_then(c, lambda a, s0, n: _sigmoid(a))

    @pl.when(j == T_SMALL // 2)
    def _():
        plain_then(0, ident, os_ref)


def _proj_call(xa, scl, shl, scc, shc, w, b, cw, cb, cos, sin, *, nb, tb, ctx_len, tm):
    r, d = xa.shape
    n = w.shape[1]
    tpb = tb // tm
    kern = functools.partial(_proj_kernel, tm=tm, tpb=tpb, ctx_len=ctx_len, tb=tb)
    return pl.pallas_call(
        kern,
        out_shape=(jax.ShapeDtypeStruct((r, T_SMALL * TN), BF16), jax.ShapeDtypeStruct((r, TN), F32)),
        grid=(r // tm, n // (2 * TN)),
        in_specs=_halo_specs(tm, d, r) + [
            pl.BlockSpec((1, 1, d), lambda i, j: (i // tpb, 0, 0)),
            pl.BlockSpec((1, 1, d), lambda i, j: (i // tpb, 0, 0)),
            pl.BlockSpec((1, d), lambda i, j: (0, 0)),
            pl.BlockSpec((1, d), lambda i, j: (0, 0)),
            pl.BlockSpec((d, 2 * TN), lambda i, j: (0, j)),
            pl.BlockSpec((1, 2 * TN), lambda i, j: (0, j)),
            pl.BlockSpec((LRU_CONV, 2 * TN), lambda i, j: (0, j)),
            pl.BlockSpec((1, 2 * TN), lambda i, j: (0, j)),
            pl.BlockSpec((tm, 128), lambda i, j: (i % tpb, 0)),
            pl.BlockSpec((tm, 128), lambda i, j: (i % tpb, 0)),
        ],
        out_specs=(pl.BlockSpec((tm, 2 * TN), lambda i, j: (i, jnp.minimum(j, T_SMALL // 2 - 1))),
                   pl.BlockSpec((tm, TN), lambda i, j: (i, 0))),
        scratch_shapes=[pltpu.VMEM((tm + 2 * HALO, d), BF16),
                        pltpu.VMEM((tm + 2 * HALO * len(_sub_blocks(tm)), TN), F32)],
        compiler_params=_cparams(("arbitrary", "arbitrary")),
        name="proj",
    )(xa, xa, xa, scl, shl, scc, shc, w, b, cw, cb, cos, sin)


def _bwd_index(i, n_ctx, n_tot):
    return jnp.where(i < n_ctx, n_ctx - 1 - i, n_tot + n_ctx - 1 - i)


def _tri_masks(c):
    row = lax.broadcasted_iota(jnp.int32, (c, c), 0)
    col = lax.broadcasted_iota(jnp.int32, (c, c), 1)
    incl = (row >= col, row <= col)
    strict = (row > col, row < col)
    return row, col, incl, strict, tuple(jnp.where(m, 1.0, 0.0).astype(BF16) for m in incl)


def _gla_kernel(qkf_ref, vf_ref, gf_ref, qkb_ref, vb_ref, gb_ref, wg_ref, bg_ref, of_ref, ob_ref, st_ref, *, nb):
    c = CHUNK
    hk = GLA_HEADS * GLA_DK

    @pl.when(pl.program_id(0) == 0)
    def _():
        st_ref[...] = jnp.zeros_like(st_ref)

    _, _, incl, _, tri_bf = _tri_masks(c)
    refs = ((qkf_ref, vf_ref, gf_ref, of_ref), (qkb_ref, vb_ref, gb_ref, ob_ref))
    groups = [(b, d) for b in range(nb) for d in range(2)]
    z = [_mmb(refs[d][2][b, :, d * GLA_GATE_RANK:(d + 1) * GLA_GATE_RANK], wg_ref[d]) + bg_ref[d] for b, d in groups]
    la = [(jnp.minimum(v, 0.0) - jnp.log1p(jnp.exp(-jnp.abs(v)))) * (1.0 / GLA_GATE_NORM) for v in z]
    cum = [_cumdot(tri_bf[d], v) for (b, d), v in zip(groups, la)]
    last = [v[c - 1:c, :] if d == 0 else v[0:1, :] for (b, d), v in zip(groups, cum)]
    mid = [v[c // 2 - 1:c // 2, :] if d == 0 else v[c // 2:c // 2 + 1, :] for (b, d), v in zip(groups, cum)]
    q = [refs[d][0][b, :, :hk].astype(F32) for b, d in groups]
    k = [refs[d][0][b, :, hk:].astype(F32) for b, d in groups]
    vv = [refs[d][1][b].astype(BF16) for b, d in groups]
    n = range(len(groups))
    qd = [(q[g] * jnp.exp(cum[g] - mid[g])).astype(BF16) for g in n]
    km = [(k[g] * jnp.exp(mid[g] - cum[g])).astype(BF16) for g in n]
    kdec = [(k[g] * jnp.exp(last[g] - cum[g])).astype(BF16) for g in n]
    qs = [(q[g] * jnp.exp(cum[g])).astype(BF16) for g in n]
    dec = [jnp.exp(last[g]) for g in n]
    chains = [(g, h) for g in n for h in range(GLA_HEADS)]
    ksl = lambda h: slice(h * GLA_DK, (h + 1) * GLA_DK)
    vsl = lambda h: slice(h * GLA_DV, (h + 1) * GLA_DV)
    st = [st_ref[groups[g][0], groups[g][1], h] for g, h in chains]
    sc = [jnp.where(incl[groups[g][1]], _mm_nt(qd[g][:, ksl(h)], km[g][:, ksl(h)]), 0.0).astype(BF16)
          for g, h in chains]
    o_in = [_mm_nt(qs[g][:, ksl(h)], s.astype(BF16)) for (g, h), s in zip(chains, st)]
    o = [_mm(s, vv[g][:, vsl(h)]) + oi for (g, h), s, oi in zip(chains, sc, o_in)]
    ut = [_mm_tn(vv[g][:, vsl(h)], kdec[g][:, ksl(h)]) for g, h in chains]
    for (g, h), s, u in zip(chains, st, ut):
        st_ref[groups[g][0], groups[g][1], h] = s * dec[g][:, ksl(h)] + u
    for g, (b, d) in enumerate(groups):
        refs[d][3][b] = jnp.concatenate(o[g * GLA_HEADS:(g + 1) * GLA_HEADS], axis=1).astype(refs[d][3].dtype)


def _chunk_specs(nb, width, col, rowf):
    return pl.BlockSpec((nb, CHUNK, width), lambda i: (0, rowf(i), col))


def _gla_call(p3, ps3, wg, bg, *, nb, tb, ctx_len):
    nct, ncc = tb // CHUNK, ctx_len // CHUNK
    fwd = lambda i: i
    bwd = lambda i: _bwd_index(i, ncc, nct)
    width = GLA_HEADS * GLA_DV
    hk = GLA_HEADS * GLA_DK

    def specs(rowf):
        return [_chunk_specs(nb, TN, T_GLA_QK, rowf), _chunk_specs(nb, TN, T_GLA_V, rowf),
                _chunk_specs(nb, 128, 0, rowf)]
    out = jax.ShapeDtypeStruct((nb, tb, width), BF16)
    return pl.pallas_call(
        functools.partial(_gla_kernel, nb=nb),
        out_shape=(out, out),
        grid=(nct,),
        in_specs=specs(fwd) + specs(bwd) + [
            pl.BlockSpec((2, GLA_GATE_RANK, hk), lambda i: (0, 0, 0)),
            pl.BlockSpec((2, 1, hk), lambda i: (0, 0, 0))],
        out_specs=(_chunk_specs(nb, width, 0, fwd), _chunk_specs(nb, width, 0, bwd)),
        scratch_shapes=[pltpu.VMEM((nb, 2, GLA_HEADS, GLA_DV, GLA_DK), F32)],
        compiler_params=_cparams(("arbitrary",)),
        name="gla",
    )(p3, p3, ps3, p3, p3, ps3, wg, bg)


def _lru_kernel(xf_ref, xb_ref, w_ref, gb_ref, lam_ref, of_ref, ob_ref, a_s, b_s, o_s, h_s, *, nb, tl):
    @pl.when(pl.program_id(0) == 0)
    def _():
        h_s[...] = jnp.zeros_like(h_s)

    chains = [(b, d) for b in range(nb) for d in range(2)]
    for n, (b, d) in enumerate(chains):
        xb = (xf_ref, xb_ref)[d][b]
        x = xb.astype(F32)
        gt = _mm(xb, w_ref[d]) + gb_ref[d]
        r = _sigmoid(gt[:, :LRU_WIDTH])
        ig = _sigmoid(gt[:, LRU_WIDTH:])
        log_a = (-LRU_C) * r * _softplus(-lam_ref[d])
        a = jnp.exp(log_a)
        a_s[n] = a
        b_s[n] = jnp.sqrt(1.0 - a * a) * ig * x

    def body(s, hs):
        out = []
        for n, (b, d) in enumerate(chains):
            rr = s if d == 0 else tl - 1 - s
            h = a_s[n, pl.ds(rr, 1), :] * hs[n] + b_s[n, pl.ds(rr, 1), :]
            o_s[n, pl.ds(rr, 1), :] = h
            out.append(h)
        return tuple(out)
    hs = lax.fori_loop(0, tl, body, tuple(h_s[n] for n in range(len(chains))), unroll=8)
    for n, (b, d) in enumerate(chains):
        h_s[n] = hs[n]
        (of_ref, ob_ref)[d][b] = o_s[n].astype(of_ref.dtype)


def _lru_call(p3, wbd, gb, lam, *, nb, tb, ctx_len, tl):
    ntl, nctl = tb // tl, ctx_len // tl
    bwd = lambda i: _bwd_index(i, nctl, ntl)
    out = jax.ShapeDtypeStruct((nb, tb, LRU_WIDTH), BF16)
    const = lambda a: pl.BlockSpec(a.shape, lambda i: (0,) * a.ndim)
    return pl.pallas_call(
        functools.partial(_lru_kernel, nb=nb, tl=tl),
        out_shape=(out, out),
        grid=(ntl,),
        in_specs=[pl.BlockSpec((nb, tl, TN), lambda i: (0, i, T_LRU_X)),
                  pl.BlockSpec((nb, tl, TN), lambda i: (0, bwd(i), T_LRU_X)),
                  const(wbd), const(gb), const(lam)],
        out_specs=(pl.BlockSpec((nb, tl, LRU_WIDTH), lambda i: (0, i, 0)),
                   pl.BlockSpec((nb, tl, LRU_WIDTH), lambda i: (0, bwd(i), 0))),
        scratch_shapes=[pltpu.VMEM((2 * nb, tl, LRU_WIDTH), F32)] * 3 + [pltpu.VMEM((2 * nb, 1, LRU_WIDTH), F32)],
        compiler_params=_cparams(("arbitrary",)),
        name="lru",
    )(p3, p3, wbd, gb, lam)


def _unit_tri_inverses(a_list, row, col):
    eye = jnp.where(row == col, 1.0, 0.0)
    blk = lambda s: (row // s) == (col // s)
    in8 = blk(8)
    a0 = [jnp.where(in8, a, 0.0) for a in a_list]
    p2 = [_mmb(x, x) for x in a0]
    ia = [eye - x for x in a0]
    t = [x + _mmb(x, p) for x, p in zip(ia, p2)]
    p4 = [_mmb(p, p) for p in p2]
    t = [x + _mmb(x, p) for x, p in zip(t, p4)]
    s = 8
    while s < CHUNK:
        sel = blk(2 * s) & jnp.logical_not(blk(s))
        y = [_mmb(jnp.where(sel, a, 0.0), x) for a, x in zip(a_list, t)]
        t = [x - _mmb(x, v) for x, v in zip(t, y)]
        s *= 2
    return t


def _gdn_kernel(qf_ref, kf_ref, vf_ref, gf_ref, qb_ref, kb_ref, vb_ref, gb_ref, alog_ref, dtb_ref,
                of_ref, ob_ref, s_ref, *, nb):
    c = CHUNK

    @pl.when(pl.program_id(0) == 0)
    def _():
        s_ref[...] = jnp.zeros_like(s_ref)

    row, col, incl, strict, tri_bf = _tri_masks(c)
    refs = (((qf_ref, kf_ref, vf_ref), gf_ref, of_ref), ((qb_ref, kb_ref, vb_ref), gb_ref, ob_ref))
    neg_a = -jnp.exp(alog_ref[...])
    groups = [(b, d) for b in range(nb) for d in range(2)]
    gblk = [refs[d][1][b] for b, d in groups]
    g_all = [neg_a * _softplus(v + dtb_ref[...]) for v in gblk]
    beta_all = [_sigmoid(v) for v in gblk]
    gc_all = [_cumdot(tri_bf[d], v) for (b, d), v in zip(groups, g_all)]
    gc_t = [v.T for v in gc_all]

    chains = [(g, h) for g in range(len(groups)) for h in range(GDN_HEADS)]
    dirn = lambda g: groups[g][1]
    lane_a = lambda g, h: SM_A + dirn(g) * GDN_HEADS + h
    lane_b = lambda g, h: SM_B + dirn(g) * GDN_HEADS + h
    gcol = [gc_all[g][:, lane_a(g, h):lane_a(g, h) + 1] for g, h in chains]
    grow = [gc_t[g][lane_a(g, h):lane_a(g, h) + 1, :] for g, h in chains]
    beta = [beta_all[g][:, lane_b(g, h):lane_b(g, h) + 1] for g, h in chains]
    glast = [v[c - 1:c, :] if dirn(g) == 0 else v[0:1, :] for (g, h), v in zip(chains, gcol)]
    decay = [jnp.exp(jnp.where(incl[dirn(g)], gc - gr, -jnp.inf)) for (g, h), gc, gr in zip(chains, gcol, grow)]

    def head(g, h, part):
        b, d = groups[g]
        return refs[d][0][part][b, :, h * GDN_DK:(h + 1) * GDN_DK]
    k_bf = [head(g, h, 1) for g, h in chains]
    qh = [head(g, h, 0).astype(F32) for g, h in chains]
    kh = [k.astype(F32) for k in k_bf]
    vh = [head(g, h, 2).astype(F32) for g, h in chains]
    kb = [k * bt for k, bt in zip(kh, beta)]
    kq = [_mm_nt(jnp.concatenate([x.astype(BF16), y.astype(BF16)], axis=0), k) for x, y, k in zip(kb, qh, k_bf)]
    a = [jnp.where(strict[dirn(g)], x[:c] * dc, 0.0) for (g, h), x, dc in zip(chains, kq, decay)]
    attn = [jnp.where(incl[dirn(g)], x[c:] * dc, 0.0).astype(BF16) for (g, h), x, dc in zip(chains, kq, decay)]
    t = _unit_tri_inverses(a, row, col)
    egc = [jnp.exp(v) for v in gcol]
    sol = [_mmb(x, jnp.concatenate([v * bt, k * e], axis=1)) for x, v, bt, k, e in zip(t, vh, beta, kb, egc)]
    kdec = [(k * jnp.exp(gl - gc)).astype(BF16) for k, gl, gc in zip(kh, glast, gcol)]
    qdec = [(x * e).astype(BF16) for x, e in zip(qh, egc)]
    s = [s_ref[groups[g][0], groups[g][1], h] for g, h in chains]
    s_bf = [v.astype(BF16) for v in s]
    ws = [_mm(jnp.concatenate([x[:, GDN_DV:].astype(BF16), qd], axis=0), sb) for x, qd, sb in zip(sol, qdec, s_bf)]
    v_new = [x[:, :GDN_DV] - y[:c] for x, y in zip(sol, ws)]
    vn_bf = [v.astype(BF16) for v in v_new]
    o = [y[c:] + _mm(at, vn) for y, at, vn in zip(ws, attn, vn_bf)]
    for (g, h), sv, gl, kd, vn in zip(chains, s, glast, kdec, vn_bf):
        s_ref[groups[g][0], groups[g][1], h] = sv * jnp.exp(gl) + _mm_tn(kd, vn)
    for g, (b, d) in enumerate(groups):
        refs[d][2][b] = jnp.concatenate(o[g * GDN_HEADS:(g + 1) * GDN_HEADS], axis=1).astype(refs[d][2].dtype)


def _gdn_call(p3, ps3, alog, dtb, *, nb, tb, ctx_len):
    nct, ncc = tb // CHUNK, ctx_len // CHUNK
    fwd = lambda i: i
    bwd = lambda i: _bwd_index(i, ncc, nct)
    width = GDN_HEADS * GDN_DV

    def specs(rowf):
        return [_chunk_specs(nb, TN, t, rowf) for t in (T_GDN_Q, T_GDN_K, T_GDN_V)] + [_chunk_specs(nb, 128, 0, rowf)]
    out = jax.ShapeDtypeStruct((nb, tb, width), BF16)
    return pl.pallas_call(
        functools.partial(_gdn_kernel, nb=nb),
        out_shape=(out, out),
        grid=(nct,),
        in_specs=specs(fwd) + specs(bwd) + [pl.BlockSpec((1, 128), lambda i: (0, 0)),
                                            pl.BlockSpec((1, 128), lambda i: (0, 0))],
        out_specs=(_chunk_specs(nb, width, 0, fwd), _chunk_specs(nb, width, 0, bwd)),
        scratch_shapes=[pltpu.VMEM((nb, 2, GDN_HEADS, GDN_DK, GDN_DV), F32)],
        compiler_params=_cparams(("arbitrary",)),
        name="gdn",
    )(p3, p3, p3, ps3, p3, p3, p3, ps3, alog, dtb)


def _na_kernel(q_ref, k0_ref, k1_ref, k2_ref, v0_ref, v1_ref, v2_ref, kc_ref, vc_ref, bias_ref, o_ref,
               kbuf, vbuf, vcbuf, *, rows):
    jj = pl.program_id(1)
    tq = NA_ROWS * GRID_W
    hd = NA_HD

    @pl.when(jj == 0)
    def _():
        outs = []
        for h in range(NA_HEADS):
            sl = slice(h * hd, (h + 1) * hd)
            s = _mm_nt(q_ref[:, sl], k1_ref[:, sl])
            m = jnp.max(s, axis=-1, keepdims=True)
            e = jnp.exp(s - m)
            l = jnp.sum(e, axis=-1, keepdims=True)
            outs.append(_mm(e.astype(BF16), v1_ref[:, sl]) / l)
        o_ref[...] = jnp.concatenate(outs, axis=1).astype(o_ref.dtype)

    @pl.when(jj > 0)
    def _():
        jp = jj - 1
        pw = 2 * hd
        pairs = range(NA_HEADS // 2)
        ones = jnp.ones((tq, pw), BF16)
        for n, (kr, vr) in enumerate(((k0_ref, v0_ref), (k1_ref, v1_ref), (k2_ref, v2_ref))):
            kbuf[n * tq:(n + 1) * tq, :] = kr[...]
            for pr in pairs:
                vbuf[n * tq:(n + 1) * tq, 2 * pr * pw:(2 * pr + 1) * pw] = vr[:, pr * pw:(pr + 1) * pw]
                vbuf[n * tq:(n + 1) * tq, (2 * pr + 1) * pw:(2 * pr + 2) * pw] = ones
        for pr in pairs:
            vcbuf[:, 2 * pr * pw:(2 * pr + 1) * pw] = vc_ref[:, pr * pw:(pr + 1) * pw]
            vcbuf[:, (2 * pr + 1) * pw:(2 * pr + 2) * pw] = ones[:vcbuf.shape[0]]
        first = lax.broadcasted_iota(jnp.int32, (1, pw), 1) < hd
        for a in range(NA_ROWS):
            r = NA_ROWS * jp + a
            rs = jnp.clip(r - NA_WIN_R // 2, 0, rows - NA_WIN_R)
            start = pl.multiple_of((rs - NA_ROWS * (jp - 1)) * GRID_W, GRID_W)
            e_idx = r - rs
            win = pl.ds(start, NA_WIN_R * GRID_W)
            qp = [q_ref[a * GRID_W:(a + 1) * GRID_W, pr * pw:(pr + 1) * pw] for pr in pairs]
            zero = jnp.zeros_like(qp[0])
            q2 = [jnp.concatenate([jnp.where(first, x, zero), jnp.where(first, zero, x)], axis=0) for x in qp]
            s_loc = [_mm_nt(q2[pr], kbuf[win, pr * pw:(pr + 1) * pw]) + bias_ref[e_idx, pr] for pr in pairs]
            s_ctx = [_mm_nt(q2[pr], kc_ref[:, pr * pw:(pr + 1) * pw]) for pr in pairs]
            nk = s_ctx[0].shape[1]
            m = []
            for pr in pairs:
                mx = s_ctx[pr]
                for k0 in range(0, s_loc[pr].shape[1], nk):
                    mx = jnp.maximum(mx, s_loc[pr][:, k0:k0 + nk])
                m.append(jnp.max(mx, axis=-1, keepdims=True))
            p_loc = [jnp.exp(s_loc[pr] - m[pr]).astype(BF16) for pr in pairs]
            p_ctx = [jnp.exp(s_ctx[pr] - m[pr]).astype(BF16) for pr in pairs]
            oe = [_mm(p_loc[pr], vbuf[win, 2 * pr * pw:(2 * pr + 2) * pw])
                  + _mm(p_ctx[pr], vcbuf[:, 2 * pr * pw:(2 * pr + 2) * pw]) for pr in pairs]
            on = [x[:, :pw] / x[:, pw:] for x in oe]
            o = [jnp.where(first, x[:GRID_W], x[GRID_W:]) for x in on]
            o_ref[a * GRID_W:(a + 1) * GRID_W, :] = jnp.concatenate(o, axis=1).astype(o_ref.dtype)


def _na_call(p, bias, *, nb, tb, ctx_len):
    r = p.shape[0]
    tq = NA_ROWS * GRID_W
    assert ctx_len == tq, "context block must be exactly one query block"
    nblk = tb // tq
    nlb = nblk - 1
    rows = (tb - ctx_len) // GRID_W
    assert rows >= NA_WIN_R and rows % NA_ROWS == 0
    width = NA_HEADS * NA_HD
    cur = lambda b, j: b * nblk + j
    prv = lambda b, j: b * nblk + 1 + jnp.clip(j - 2, 0, nlb - 1)
    nxt = lambda b, j: b * nblk + 1 + jnp.clip(j, 0, nlb - 1)
    blk = lambda rowf, t: pl.BlockSpec((tq, TN), lambda b, j: (rowf(b, j), t))
    return pl.pallas_call(
        functools.partial(_na_kernel, rows=rows),
        out_shape=jax.ShapeDtypeStruct((r, width), BF16),
        grid=(nb, nblk),
        in_specs=[blk(cur, T_NA_Q),
                  blk(prv, T_NA_K), blk(cur, T_NA_K), blk(nxt, T_NA_K),
                  blk(prv, T_NA_V), blk(cur, T_NA_V), blk(nxt, T_NA_V),
                  pl.BlockSpec((ctx_len, TN), lambda b, j: (b * nblk, T_NA_K)),
                  pl.BlockSpec((ctx_len, TN), lambda b, j: (b * nblk, T_NA_V)),
                  pl.BlockSpec(bias.shape, lambda b, j: (0, 0, 0, 0), pipeline_mode=pl.Buffered(1))],
        out_specs=pl.BlockSpec((tq, width), lambda b, j: (cur(b, j), 0)),
        scratch_shapes=[pltpu.VMEM((3 * tq, TN), BF16), pltpu.VMEM((3 * tq, 2 * TN), BF16),
                        pltpu.VMEM((ctx_len, 2 * TN), BF16)],
        compiler_params=_cparams(("arbitrary", "arbitrary")),
        name="na",
    )(p, p, p, p, p, p, p, p, p, bias)


def _na_bias_table(rpb):
    c = np.arange(GRID_W)
    cs = np.clip(c - NA_WIN_C // 2, 0, GRID_W - NA_WIN_C)
    kc = np.arange(GRID_W)
    inwin = (kc[None, :] >= cs[:, None]) & (kc[None, :] < cs[:, None] + NA_WIN_C)
    dc = np.clip(kc[None, :] - c[:, None] + NA_WIN_C - 1, 0, 2 * NA_WIN_C - 2)
    onehot = np.zeros((2 * NA_WIN_C - 1, GRID_W * GRID_W), np.float32)
    onehot[dc.reshape(-1), np.arange(GRID_W * GRID_W)] = 1.0
    byrow = jnp.einsum('hrd,dm->hrm', rpb, onehot, precision=lax.Precision.HIGHEST)
    byrow = jnp.where(inwin[None, None], byrow.reshape(NA_HEADS, 2 * NA_WIN_R - 1, GRID_W, GRID_W), NEG_BIG)
    tabs = [jnp.transpose(byrow[:, NA_WIN_R - 1 - e:2 * NA_WIN_R - 1 - e], (0, 2, 1, 3))
            .reshape(NA_HEADS, GRID_W, NA_WIN_R * GRID_W) for e in range(NA_WIN_R)]
    return jnp.stack(tabs).reshape(NA_WIN_R, NA_HEADS // 2, 2 * GRID_W, NA_WIN_R * GRID_W)


def _head_rms(o, g, width):
    outs = []
    for h in range(o.shape[1] // width):
        xs = o[:, h * width:(h + 1) * width]
        ms = jnp.mean(xs * xs, axis=-1, keepdims=True)
        outs.append(xs * lax.rsqrt(ms + NORM_EPS) * g)
    return jnp.concatenate(outs, axis=1)


def _merge_kernel(gof_ref, gob_ref, gg_ref, lhf_ref, lhb_ref, ly_ref, dof_ref, dob_ref, dz_ref, na_ref,
                  m0_ref, m1_ref, m2_ref, m3_ref, x_ref, g1l_ref, g1c_ref, wbr_ref, wout_ref, bout_ref,
                  gng_ref, gnd_ref, lng_ref, lnb_ref, o_ref, *, tm, tpb, ctx_len, alpha):
    p0 = (pl.program_id(0) % tpb) * tm
    for s0, ns in _sub_blocks(tm):
        rows = slice(s0, s0 + ns)
        f = lambda ref: ref[rows, :].astype(F32)
        y_a = _head_rms(f(gof_ref) + f(gob_ref), gng_ref[...], GLA_DV) * _silu(f(gg_ref))
        y_b = (f(lhf_ref) + f(lhb_ref)) * _gelu_tanh(f(ly_ref))
        y_c = _head_rms(f(dof_ref) + f(dob_ref), gnd_ref[...], GDN_DV) * _silu(f(dz_ref))
        y_d = na_ref[rows, :]
        m = None
        for n, (y, g_ref) in enumerate(((y_a, m0_ref), (y_b, m1_ref), (y_c, m2_ref), (y_d, m3_ref))):
            term = f(g_ref) * _mm(y.astype(BF16), wbr_ref[n])
            m = term if m is None else m + term
        out = _mm(m.astype(BF16), wout_ref[...]) + bout_ref[...]
        is_ctx = _row_pos(ns, p0 + s0) < ctx_len
        g1 = jnp.where(is_ctx, g1c_ref[...], g1l_ref[0])
        o_ref[rows, :] = _layer_norm(alpha * x_ref[rows, :] + g1 * out, lng_ref[...], lnb_ref[...])


def _merge_call(p, gof, gob, lhf, lhb, dof, dob, na, xa, g1l, g1c, wbr, wout, bout, gng, gnd, lng, lnb,
                *, tb, ctx_len, tm, alpha):
    r, d = xa.shape
    tpb = tb // tm
    bw = BRANCH_W
    row = lambda w: pl.BlockSpec((tm, w), lambda i: (i, 0))
    ptile = lambda t, w: pl.BlockSpec((tm, w), lambda i: (i, t * TN // w))
    const = lambda a: pl.BlockSpec(a.shape, lambda i: (0,) * a.ndim)
    kern = functools.partial(_merge_kernel, tm=tm, tpb=tpb, ctx_len=ctx_len, alpha=alpha)
    return pl.pallas_call(
        kern,
        out_shape=jax.ShapeDtypeStruct((r, d), F32),
        grid=(r // tm,),
        in_specs=[row(bw), row(bw), ptile(T_GLA_G, bw),
                  row(bw), row(bw),
                  ptile(T_LRU_Y, bw),
                  row(bw), row(bw), ptile(T_GDN_Z, bw),
                  row(bw),
                  pl.BlockSpec((tm, d), lambda i: (i, T_MERGE * TN // d + 0)),
                  pl.BlockSpec((tm, d), lambda i: (i, T_MERGE * TN // d + 1)),
                  pl.BlockSpec((tm, d), lambda i: (i, T_MERGE * TN // d + 2)),
                  pl.BlockSpec((tm, d), lambda i: (i, T_MERGE * TN // d + 3)),
                  row(d),
                  pl.BlockSpec((1, 1, d), lambda i: (i // tpb, 0, 0)),
                  const(g1c), const(wbr), const(wout), const(bout), const(gng), const(gnd), const(lng), const(lnb)],
        out_specs=row(d),
        compiler_params=_cparams(("arbitrary",)),
        name="merge",
    )(gof, gob, p, lhf, lhb, p, dof, dob, p, na, p, p, p, p, xa, g1l, g1c, wbr, wout, bout, gng, gnd, lng, lnb)


def _ffn_kernel(x_ref, xp_ref, xn_ref, scl_ref, shl_ref, gl_ref, scc_ref, shc_ref, gc_ref, wa_ref, wb_ref,
                ba_ref, bb_ref, cw_ref, cb_ref, wd_ref, bd_ref, lng_ref, lnb_ref, o_ref,
                lhs_ref, a_ref, h_ref, *, tm, tpb, ctx_len, tb, nc, tc, alpha):
    i = pl.program_id(0)
    c = pl.program_id(1)
    p0 = (i % tpb) * tm

    @pl.when(c == 0)
    def _():
        _build_lhs(lhs_ref, x_ref, xp_ref, xn_ref, scl_ref[0], shl_ref[0], scc_ref[...], shc_ref[...],
                   p0, tm, ctx_len)

    left = FFN_CONV // 2
    subs = _sub_blocks(tm)
    for s0, n in subs:
        a0 = _acc_row(s0, tm)
        a_ref[a0 - HALO:a0 + n + HALO, :] = _mm(lhs_ref[s0:s0 + n + 2 * HALO, :], wa_ref[...]) + ba_ref[...]
    gates = [_mm(lhs_ref[HALO + s0:HALO + s0 + n, :], wb_ref[...]) + bb_ref[...] for s0, n in subs]
    cw = cw_ref[...]
    hidden = [(_silu(_dwconv(a_ref, cw, left, _acc_row(s0, tm), n) + cb_ref[...]) * g).astype(BF16)
              for (s0, n), g in zip(subs, gates)]

    def gate_rows(r0):
        n = subs[0][1]
        return gates[r0 // n][r0 % n:r0 % n + HALO, :]
    fixes = [(r0, (_silu(_dwconv(a_ref, cw, left, _acc_row(r0, tm), HALO, p0 + r0, ctx_len, tb) + cb_ref[...])
                   * gate_rows(r0)).astype(BF16)) for r0 in _segment_end_windows(tm, ctx_len)]
    for cc in range(nc):
        @pl.when(c == cc)
        def _(cc=cc):
            for (s0, n), h in zip(subs, hidden):
                h_ref[s0:s0 + n, cc * tc:(cc + 1) * tc] = h
            for r0, fixed in fixes:
                h_ref[r0:r0 + HALO, cc * tc:(cc + 1) * tc] = fixed

    @pl.when(c == nc - 1)
    def _():
        for s0, n in subs:
            f = _mm(h_ref[s0:s0 + n, :], wd_ref[...]) + bd_ref[...]
            is_ctx = _row_pos(n, p0 + s0) < ctx_len
            g2 = jnp.where(is_ctx, gc_ref[...], gl_ref[0])
            o_ref[s0:s0 + n, :] = _layer_norm(alpha * x_ref[s0:s0 + n, :] + g2 * f, lng_ref[...], lnb_ref[...])


def _ffn_call(xa, scl, shl, gl, scc, shc, gc, w_up, b_up, cw, cb, w_down, b_down, lng, lnb,
              *, tb, ctx_len, tm, tc, alpha):
    r, d = xa.shape
    dff = w_down.shape[0]
    nc = dff // tc
    tpb = tb // tm
    lat = pl.BlockSpec((1, 1, d), lambda i, c: (i // tpb, 0, 0))
    vec = pl.BlockSpec((1, d), lambda i, c: (0, 0))
    kern = functools.partial(_ffn_kernel, tm=tm, tpb=tpb, ctx_len=ctx_len, tb=tb, nc=nc, tc=tc, alpha=alpha)
    return pl.pallas_call(
        kern,
        out_shape=jax.ShapeDtypeStruct((r, d), F32),
        grid=(r // tm, nc),
        in_specs=_halo_specs(tm, d, r) + [
            lat, lat, lat, vec, vec, vec,
            pl.BlockSpec((d, tc), lambda i, c: (0, c)),
            pl.BlockSpec((d, tc), lambda i, c: (0, nc + c)),
            pl.BlockSpec((1, tc), lambda i, c: (0, c)),
            pl.BlockSpec((1, tc), lambda i, c: (0, nc + c)),
            pl.BlockSpec((FFN_CONV, tc), lambda i, c: (0, c)),
            pl.BlockSpec((1, tc), lambda i, c: (0, c)),
            pl.BlockSpec((dff, d), lambda i, c: (0, 0), pipeline_mode=pl.Buffered(1)),
            vec, vec, vec],
        out_specs=pl.BlockSpec((tm, d), lambda i, c: (i, 0)),
        scratch_shapes=[pltpu.VMEM((tm + 2 * HALO, d), BF16),
                        pltpu.VMEM((tm + 2 * HALO * len(_sub_blocks(tm)), tc), F32),
                        pltpu.VMEM((tm, dff), BF16)],
        compiler_params=_cparams(("arbitrary", "arbitrary")),
        name="ffn",
    )(xa, xa, xa, scl, shl, gl, scc, shc, gc, w_up, w_up, b_up, b_up, cw, cb, w_down, b_down, lng, lnb)


def _pad_cols(a, width):
    return jnp.pad(a, ((0, 0), (0, width - a.shape[1])))


def _in_proj_params(w_in, b_in, lru_conv_w, lru_conv_b, gdn_conv_w):
    sizes = (GLA_HEADS * GLA_DK, GLA_HEADS * GLA_DK, GLA_HEADS * GLA_DV, GLA_HEADS * GLA_DV,
             GLA_GATE_RANK, GLA_GATE_RANK, LRU_WIDTH, LRU_WIDTH,
             GDN_HEADS * GDN_DK, GDN_HEADS * GDN_DK, GDN_HEADS * GDN_DV, GDN_HEADS * GDN_DV,
             2 * GDN_HEADS, 2 * GDN_HEADS, NA_HEADS * NA_HD, NA_HEADS * NA_HD, NA_HEADS * NA_HD)
    names = ('gla_q', 'gla_k', 'gla_v', 'gla_g', 'gla_fw', 'gla_bw', 'lru_x', 'lru_y', 'gdn_q', 'gdn_k',
             'gdn_v', 'gdn_z', 'gdn_a', 'gdn_b', 'na_q', 'na_k', 'na_v')
    offs = np.concatenate([[0], np.cumsum(sizes)])

    def reorder(m):
        col = {n: m[:, offs[k]:offs[k + 1]] for k, n in enumerate(names)}
        col['merge'] = m[:, offs[-1]:]
        small = _pad_cols(jnp.concatenate([col['gla_fw'], col['gla_bw'], col['gdn_a'], col['gdn_b']], axis=1), TN)
        tiles = {T_GLA_QK: jnp.concatenate([col['gla_q'] * (GLA_DK ** -0.5), col['gla_k']], axis=1),
                 T_GLA_V: col['gla_v'], T_GLA_G: col['gla_g'], T_GDN_Q: col['gdn_q'], T_GDN_K: col['gdn_k'],
                 T_GDN_V: col['gdn_v'], T_GDN_Z: col['gdn_z'], T_LRU_X: col['lru_x'], T_LRU_Y: col['lru_y'],
                 T_NA_Q: col['na_q'] * (NA_HD ** -0.5), T_NA_K: col['na_k'], T_NA_V: col['na_v'],
                 T_SMALL: _pad_cols(small, 2 * TN)}
        order = [tiles[t] for t in range(T_MERGE)] + [col['merge'], tiles[T_SMALL]]
        return jnp.concatenate(order, axis=1)
    w = reorder(w_in.astype(BF16))
    b = reorder(b_in[None, :])
    n = w.shape[1]
    assert n == N_TILES * TN
    cw = jnp.zeros((LRU_CONV, n), F32)
    cw = cw.at[:, T_LRU_X * TN:(T_LRU_X + 1) * TN].set(lru_conv_w)
    for k, t in enumerate((T_GDN_Q, T_GDN_K, T_GDN_V)):
        cw = cw.at[:, t * TN:(t + 1) * TN].set(gdn_conv_w[:, k * TN:(k + 1) * TN])
    cb = jnp.zeros((1, n), F32).at[0, T_LRU_X * TN:(T_LRU_X + 1) * TN].set(lru_conv_b)
    return w, b, cw, cb


def _rope_tables(ctx_len, seq_len):
    quarter = GLA_DK // 4
    t = jnp.arange(seq_len)
    inv = ROPE_BASE ** (-jnp.arange(quarter, dtype=F32) / quarter)
    ang = jnp.concatenate([(t // GRID_W).astype(F32)[:, None] * inv,
                           (t % GRID_W).astype(F32)[:, None] * inv], -1)
    cos = jnp.concatenate([jnp.ones((ctx_len, 2 * quarter), F32), jnp.cos(ang)], axis=0)
    sin = jnp.concatenate([jnp.zeros((ctx_len, 2 * quarter), F32), jnp.sin(ang)], axis=0)
    return jnp.concatenate([cos, cos, cos, cos], axis=1), jnp.concatenate([-sin, sin, -sin, sin], axis=1)


def _lru_gate_matrix(gate_w):
    eye = jnp.eye(LRU_BLOCKS, dtype=F32)
    dense = jnp.einsum('xgnde,nm->xndgme', gate_w, eye)
    return dense.reshape(2, LRU_WIDTH, 2 * LRU_WIDTH).astype(BF16)


def _small_lane_vec(v):
    return jnp.zeros((1, 128), F32).at[0, SM_A:SM_A + 2 * GDN_HEADS].set(v.reshape(-1))


def _row_tile(tb, target):
    best = HALO
    for t in range(HALO, target + 1, HALO):
        if tb % t == 0:
            best = t
    return best


def kernel(x, c, ctx, c_ctx, w_mod, b_mod, w_in, b_in, gla_w_gate, gla_b_gate, gla_norm, lru_conv_w, lru_conv_b,
           lru_gate_w, lru_gate_b, lru_lambda, gdn_conv_w, gdn_a_log, gdn_dt_bias, gdn_norm, na_rpb, w_branch,
           w_out, b_out, ln1_g, ln1_b, ffn_w_up, ffn_b_up, ffn_conv_w, ffn_conv_b, ffn_w_down, ffn_b_down,
           ln2_g, ln2_b):
    nb, seq_len, d = x.shape
    ctx_len = ctx.shape[1]
    depth = w_mod.shape[0]
    tb = ctx_len + seq_len
    alpha = (2 * depth) ** 0.25
    tm = _row_tile(tb, ROW_TILE)
    tl = NA_ROWS * GRID_W
    dims = dict(nb=nb, tb=tb, ctx_len=ctx_len)

    xa = jnp.concatenate([ctx, x], axis=1).reshape(nb * tb, d)
    crows = jnp.zeros((8, d), F32).at[:nb].set(c).at[nb].set(c_ctx)
    mods = _mod_call(crows, w_mod, b_mod)
    cos, sin = _rope_tables(ctx_len, seq_len)

    for l in range(depth):
        lat = [mods[l, :nb, k * d:(k + 1) * d].reshape(nb, 1, d) for k in range(6)]
        cx = [mods[l, nb:nb + 1, k * d:(k + 1) * d] for k in range(6)]
        w, b, cw, cb = _in_proj_params(w_in[l], b_in[l], lru_conv_w[l], lru_conv_b[l], gdn_conv_w[l])
        p, ps = _proj_call(xa, lat[1], lat[0], cx[1], cx[0], w, b, cw, cb, cos, sin, tm=tm, **dims)

        p3, ps3 = p.reshape(nb, tb, -1), ps.reshape(nb, tb, -1)
        flat = lambda a: a.reshape(nb * tb, -1)
        gof, gob = map(flat, _gla_call(p3, ps3, gla_w_gate[l], gla_b_gate[l].reshape(2, 1, -1), **dims))
        lhf, lhb = map(flat, _lru_call(p3, _lru_gate_matrix(lru_gate_w[l]), lru_gate_b[l].reshape(2, 1, -1),
                                       lru_lambda[l].reshape(2, 1, -1), tl=tl, **dims))
        dof, dob = map(flat, _gdn_call(p3, ps3, _small_lane_vec(gdn_a_log[l]), _small_lane_vec(gdn_dt_bias[l]),
                                       **dims))
        na = _na_call(p, _na_bias_table(na_rpb[l]), **dims)

        xa = _merge_call(p, gof, gob, lhf, lhb, dof, dob, na, xa, lat[2], cx[2],
                         w_branch[l].astype(BF16), w_out[l].astype(BF16), b_out[l].reshape(1, d),
                         gla_norm[l].reshape(1, -1),
                         gdn_norm[l].reshape(1, -1), ln1_g[l].reshape(1, d), ln1_b[l].reshape(1, d),
                         tb=tb, ctx_len=ctx_len, tm=_row_tile(tb, MERGE_TILE), alpha=alpha)
        xa = _ffn_call(xa, lat[4], lat[3], lat[5], cx[4], cx[3], cx[5],
                       ffn_w_up[l].astype(BF16), ffn_b_up[l].reshape(1, -1), ffn_conv_w[l],
                       ffn_conv_b[l].reshape(1, -1), ffn_w_down[l].astype(BF16), ffn_b_down[l].reshape(1, d),
                       ln2_g[l].reshape(1, d), ln2_b[l].reshape(1, d),
                       tb=tb, ctx_len=ctx_len, tm=tm, tc=256, alpha=alpha)
    return xa.reshape(nb, tb, d)[:, ctx_len:, :]
```

```python
import functools
import math

import jax
import jax.numpy as jnp
import numpy as np
from jax import lax
from jax.experimental import pallas as pl
from jax.experimental.pallas import tpu as pltpu

F32 = jnp.float32
BF16 = jnp.bfloat16

GRID_W = 64
CHUNK = 64
GLA_HEADS, GLA_DK, GLA_DV = 4, 64, 128
GLA_GATE_RANK = 16
GLA_GATE_NORM = 16.0
ROPE_BASE = 10000.0
LRU_WIDTH, LRU_BLOCKS, LRU_CONV, LRU_C = 512, 8, 4, 8.0
GDN_HEADS, GDN_DK, GDN_DV, GDN_CONV = 4, 128, 128, 4
NA_HEADS, NA_HD, NA_WIN_R, NA_WIN_C = 8, 64, 8, 16
FFN_CONV = 3
LN_EPS = 1e-5
NORM_EPS = 1e-6
BRANCH_W = 512

HALO = 16
ROW_TILE = 1280
MERGE_TILE = 512
ROW_SUB = 4
TN = 512
NA_ROWS = 4
VMEM_LIMIT = 56 * 1024 * 1024
NEG_BIG = -1e30

T_GLA_QK, T_GLA_V = 0, 1
T_GDN_Q, T_GLA_G = 2, 3
T_GDN_K, T_GDN_Z = 4, 5
T_GDN_V, T_LRU_Y = 6, 7
T_LRU_X, T_NA_Q = 8, 9
T_NA_K, T_NA_V = 10, 11
T_MERGE = 12
T_SMALL = 20
N_TILES = 22
SMALL_W = 128
SM_FW, SM_BW, SM_A, SM_B = 0, 16, 32, 40


def _cparams(sem):
    return pltpu.CompilerParams(dimension_semantics=sem, vmem_limit_bytes=VMEM_LIMIT)


def _mm(a, b):
    return jnp.dot(a, b, preferred_element_type=F32)


def _mm_nt(a, b):
    return lax.dot_general(a, b, (((1,), (1,)), ((), ())), preferred_element_type=F32)


def _mm_tn(a, b):
    return lax.dot_general(a, b, (((0,), (0,)), ((), ())), preferred_element_type=F32)


def _mmb(a, b):
    return _mm(a.astype(BF16), b.astype(BF16))


def _cumdot(tri_bf, x):
    hi = x.astype(BF16)
    lo = (x - hi.astype(F32)).astype(BF16)
    return _mm(tri_bf, hi) + _mm(tri_bf, lo)


def _sigmoid(x):
    return 0.5 * jnp.tanh(0.5 * x) + 0.5


def _silu(x):
    return x * _sigmoid(x)


def _softplus(x):
    return jnp.maximum(x, 0.0) + jnp.log1p(jnp.exp(-jnp.abs(x)))


def _gelu_tanh(x):
    return x * (0.5 * (1.0 + jnp.tanh(math.sqrt(2.0 / math.pi) * (x + 0.044715 * (x * x * x)))))


def _layer_norm(y, g, b):
    mu = jnp.mean(y, axis=-1, keepdims=True)
    yc = y - mu
    var = jnp.mean(yc * yc, axis=-1, keepdims=True)
    return yc * lax.rsqrt(var + LN_EPS) * g + b


def _row_pos(n, off):
    return lax.broadcasted_iota(jnp.int32, (n, 1), 0) + off


def _mod_kernel(c_ref, w_ref, b_ref, o_ref):
    cv = c_ref[...]
    o_ref[0] = jnp.dot(_silu(cv), w_ref[0], precision=lax.Precision.HIGHEST,
                       preferred_element_type=F32) + b_ref[0]


def _mod_call(crows, w_mod, b_mod):
    depth, d, n = w_mod.shape
    tn = n // 4
    return pl.pallas_call(
        _mod_kernel,
        out_shape=jax.ShapeDtypeStruct((depth, 8, n), F32),
        grid=(depth, n // tn),
        in_specs=[pl.BlockSpec((8, d), lambda l, j: (0, 0)),
                  pl.BlockSpec((1, d, tn), lambda l, j: (l, 0, j)),
                  pl.BlockSpec((1, 1, tn), lambda l, j: (l, 0, j))],
        out_specs=pl.BlockSpec((1, 8, tn), lambda l, j: (l, 0, j)),
        compiler_params=_cparams(("arbitrary", "arbitrary")),
        name="mod",
    )(crows, w_mod, b_mod.reshape(depth, 1, n))


def _build_lhs(lhs_ref, x_ref, xp_ref, xn_ref, scl, shl, scc, shc, p0, tm, ctx_len):
    def mod(xv, p):
        is_ctx = (p >= 0) & (p < ctx_len)
        sc = jnp.where(is_ctx, scc, scl)
        sh = jnp.where(is_ctx, shc, shl)
        return (xv * (1.0 + sc) + sh).astype(BF16)
    lhs_ref[0:HALO, :] = mod(xp_ref[...], _row_pos(HALO, p0 - HALO))
    lhs_ref[HALO:HALO + tm, :] = mod(x_ref[...], _row_pos(tm, p0))
    lhs_ref[HALO + tm:, :] = mod(xn_ref[...], _row_pos(HALO, p0 + tm))


def _dwconv(acc_ref, cw, left, a0, n, pos=None, ctx_len=None, tb=None):
    if pos is not None:
        p = _row_pos(n, pos)
        in_ctx = p < ctx_len
        seg_lo = jnp.where(in_ctx, 0, ctx_len)
        seg_hi = jnp.where(in_ctx, ctx_len, tb)
    out = None
    if pos is None and a0 % 8 == 0 and n % 8 == 0:
        blk = acc_ref[pl.ds(a0 - HALO, n + 2 * HALO), :]
        for k in range(cw.shape[0]):
            d = k - left
            v = blk if d == 0 else pltpu.roll(blk, (-d) % blk.shape[0], 0)
            term = v[HALO:HALO + n, :] * cw[k:k + 1, :]
            out = term if out is None else out + term
        return out
    for k in range(cw.shape[0]):
        d = k - left
        v = acc_ref[pl.ds(a0 + d, n), :]
        if pos is not None:
            v = jnp.where((p + d >= seg_lo) & (p + d < seg_hi), v, 0.0)
        term = v * cw[k:k + 1, :]
        out = term if out is None else out + term
    return out


def _sub_blocks(tm):
    n = tm // ROW_SUB if tm % (ROW_SUB * HALO) == 0 else tm
    return [(s0, n) for s0 in range(0, tm, n)]


def _acc_row(r, tm):
    n = _sub_blocks(tm)[0][1]
    return (r // n) * (n + 2 * HALO) + HALO + r % n


def _segment_end_windows(tm, ctx_len):
    starts = {0, tm - HALO}
    if ctx_len % tm:
        starts |= {ctx_len % tm - HALO, ctx_len % tm}
    return sorted(starts)


def _halo_specs(tm, d, r):
    hb = tm // HALO
    return [pl.BlockSpec((tm, d), lambda i, j: (i, 0)),
            pl.BlockSpec((HALO, d), lambda i, j: (jnp.maximum(i * hb - 1, 0), 0)),
            pl.BlockSpec((HALO, d), lambda i, j: (jnp.minimum((i + 1) * hb, r // HALO - 1), 0))]


def _swap_half_heads(x):
    n = x.shape[1]
    lane = lax.broadcasted_iota(jnp.int32, x.shape, 1)
    return jnp.where((lane % 64) < 32, pltpu.roll(x, n - 32, 1), pltpu.roll(x, 32, 1))


def _l2_heads(x, width, scale):
    outs = []
    for h in range(x.shape[1] // width):
        xs = x[:, h * width:(h + 1) * width]
        ss = jnp.sum(xs * xs, axis=-1, keepdims=True)
        y = xs * lax.rsqrt(ss + NORM_EPS)
        outs.append(y * scale if scale != 1.0 else y)
    return jnp.concatenate(outs, axis=1)


def _proj_kernel(x_ref, xp_ref, xn_ref, scl_ref, shl_ref, scc_ref, shc_ref, w_ref, b_ref, cw_ref, cb_ref,
                 cos_ref, sin_ref, o_ref, os_ref, lhs_ref, acc_ref, *, tm, tpb, ctx_len, tb):
    i = pl.program_id(0)
    j = pl.program_id(1)
    p0 = (i % tpb) * tm

    @pl.when(j == 0)
    def _():
        _build_lhs(lhs_ref, x_ref, xp_ref, xn_ref, scl_ref[0], shl_ref[0], scc_ref[...], shc_ref[...],
                   p0, tm, ctx_len)

    def col(c):
        return slice(c * TN, (c + 1) * TN)

    def plain_then(c, epilogue, out_ref=o_ref):
        for s0, n in _sub_blocks(tm):
            a = _mm(lhs_ref[HALO + s0:HALO + s0 + n, :], w_ref[:, col(c)]) + b_ref[:, col(c)]
            out_ref[s0:s0 + n, col(c)] = epilogue(a, s0, n).astype(out_ref.dtype)

    def conv_then(c, epilogue):
        left = LRU_CONV // 2
        cw = cw_ref[:, col(c)]
        cb = cb_ref[:, col(c)]
        for s0, n in _sub_blocks(tm):
            a0 = _acc_row(s0, tm)
            acc_ref[a0 - HALO:a0 + n + HALO, :] = (_mm(lhs_ref[s0:s0 + n + 2 * HALO, :], w_ref[:, col(c)])
                                                   + b_ref[:, col(c)])
        for s0, n in _sub_blocks(tm):
            conv = _dwconv(acc_ref, cw, left, _acc_row(s0, tm), n) + cb
            o_ref[s0:s0 + n, col(c)] = epilogue(conv).astype(o_ref.dtype)
        for r0 in _segment_end_windows(tm, ctx_len):
            fixed = _dwconv(acc_ref, cw, left, _acc_row(r0, tm), HALO, p0 + r0, ctx_len, tb) + cb
            o_ref[r0:r0 + HALO, col(c)] = epilogue(fixed).astype(o_ref.dtype)

    def rope(a, s0, n):
        reps = a.shape[1] // cos_ref.shape[1]
        cos = jnp.concatenate([cos_ref[s0:s0 + n, :]] * reps, axis=1)
        sin = jnp.concatenate([sin_ref[s0:s0 + n, :]] * reps, axis=1)
        return a * cos + _swap_half_heads(a) * sin

    ident = lambda a, s0, n: a
    conv_epilogues = {T_GDN_Q: lambda v: _l2_heads(_silu(v), GDN_DK, GDN_DK ** -0.5),
                      T_GDN_K: lambda v: _l2_heads(_silu(v), GDN_DK, 1.0),
                      T_GDN_V: _silu,
                      T_LRU_X: lambda v: v}
    for step in range(T_MERGE // 2):
        @pl.when(j == step)
        def _(step=step):
            for c, t in enumerate((2 * step, 2 * step + 1)):
                if t in conv_epilogues:
                    conv_then(c, conv_epilogues[t])
                else:
                    plain_then(c, rope if t == T_GLA_QK else ident)

    @pl.when((j >= T_MERGE // 2) & (j < T_SMALL // 2))
    def _():
        for c in range(2):
            plain_then(c, lambda a, s0, n: _sigmoid(a))

    @pl.when(j == T_SMALL // 2)
    def _():
        for s0, n in _sub_blocks(tm):
            os_ref[s0:s0 + n, :] = (_mm(lhs_ref[HALO + s0:HALO + s0 + n, :], w_ref[:, :SMALL_W])
                                    + b_ref[:, :SMALL_W])


def _proj_call(xa, scl, shl, scc, shc, w, b, cw, cb, cos, sin, *, nb, tb, ctx_len, tm):
    r, d = xa.shape
    n = w.shape[1]
    tpb = tb // tm
    kern = functools.partial(_proj_kernel, tm=tm, tpb=tpb, ctx_len=ctx_len, tb=tb)
    return pl.pallas_call(
        kern,
        out_shape=(jax.ShapeDtypeStruct((r, T_SMALL * TN), BF16), jax.ShapeDtypeStruct((r, SMALL_W), F32)),
        grid=(r // tm, n // (2 * TN)),
        in_specs=_halo_specs(tm, d, r) + [
            pl.BlockSpec((1, 1, d), lambda i, j: (i // tpb, 0, 0)),
            pl.BlockSpec((1, 1, d), lambda i, j: (i // tpb, 0, 0)),
            pl.BlockSpec((1, d), lambda i, j: (0, 0)),
            pl.BlockSpec((1, d), lambda i, j: (0, 0)),
            pl.BlockSpec((d, 2 * TN), lambda i, j: (0, j)),
            pl.BlockSpec((1, 2 * TN), lambda i, j: (0, j)),
            pl.BlockSpec((LRU_CONV, 2 * TN), lambda i, j: (0, j)),
            pl.BlockSpec((1, 2 * TN), lambda i, j: (0, j)),
            pl.BlockSpec((tm, 128), lambda i, j: (i % tpb, 0)),
            pl.BlockSpec((tm, 128), lambda i, j: (i % tpb, 0)),
        ],
        out_specs=(pl.BlockSpec((tm, 2 * TN), lambda i, j: (i, jnp.minimum(j, T_SMALL // 2 - 1))),
                   pl.BlockSpec((tm, SMALL_W), lambda i, j: (i, 0))),
        scratch_shapes=[pltpu.VMEM((tm + 2 * HALO, d), BF16),
                        pltpu.VMEM((tm + 2 * HALO * len(_sub_blocks(tm)), TN), F32)],
        compiler_params=_cparams(("arbitrary", "arbitrary")),
        name="proj",
    )(xa, xa, xa, scl, shl, scc, shc, w, b, cw, cb, cos, sin)


def _bwd_index(i, n_ctx, n_tot):
    return jnp.where(i < n_ctx, n_ctx - 1 - i, n_tot + n_ctx - 1 - i)


def _tri_masks(c):
    row = lax.broadcasted_iota(jnp.int32, (c, c), 0)
    col = lax.broadcasted_iota(jnp.int32, (c, c), 1)
    incl = (row >= col, row <= col)
    strict = (row > col, row < col)
    return row, col, incl, strict, tuple(jnp.where(m, 1.0, 0.0).astype(BF16) for m in incl)


def _gla_step(qkf_ref, vf_ref, gf_ref, qkb_ref, vb_ref, gb_ref, wg_ref, bg_ref, of_ref, ob_ref, st_ref, nb):
    c = CHUNK
    hk = GLA_HEADS * GLA_DK
    _, _, incl, _, tri_bf = _tri_masks(c)
    refs = ((qkf_ref, vf_ref, gf_ref, of_ref), (qkb_ref, vb_ref, gb_ref, ob_ref))
    groups = [(b, d) for b in range(nb) for d in range(2)]
    z = [_mmb(refs[d][2][b, :, d * GLA_GATE_RANK:(d + 1) * GLA_GATE_RANK], wg_ref[d]) + bg_ref[d] for b, d in groups]
    la = [(jnp.minimum(v, 0.0) - jnp.log1p(jnp.exp(-jnp.abs(v)))) * (1.0 / GLA_GATE_NORM) for v in z]
    cum = [_cumdot(tri_bf[d], v) for (b, d), v in zip(groups, la)]
    last = [v[c - 1:c, :] if d == 0 else v[0:1, :] for (b, d), v in zip(groups, cum)]
    mid = [v[c // 2 - 1:c // 2, :] if d == 0 else v[c // 2:c // 2 + 1, :] for (b, d), v in zip(groups, cum)]
    q = [refs[d][0][b, :, :hk].astype(F32) for b, d in groups]
    k = [refs[d][0][b, :, hk:].astype(F32) for b, d in groups]
    vv = [refs[d][1][b].astype(BF16) for b, d in groups]
    n = range(len(groups))
    qd = [(q[g] * jnp.exp(cum[g] - mid[g])).astype(BF16) for g in n]
    km = [(k[g] * jnp.exp(mid[g] - cum[g])).astype(BF16) for g in n]
    kdec = [(k[g] * jnp.exp(last[g] - cum[g])).astype(BF16) for g in n]
    qs = [(q[g] * jnp.exp(cum[g])).astype(BF16) for g in n]
    dec = [jnp.exp(last[g]) for g in n]
    chains = [(g, h) for g in n for h in range(GLA_HEADS)]
    ksl = lambda h: slice(h * GLA_DK, (h + 1) * GLA_DK)
    vsl = lambda h: slice(h * GLA_DV, (h + 1) * GLA_DV)
    st = [st_ref[groups[g][0], groups[g][1], h] for g, h in chains]
    sc = [jnp.where(incl[groups[g][1]], _mm_nt(qd[g][:, ksl(h)], km[g][:, ksl(h)]), 0.0).astype(BF16)
          for g, h in chains]
    o_in = [_mm_nt(qs[g][:, ksl(h)], s.astype(BF16)) for (g, h), s in zip(chains, st)]
    o = [_mm(s, vv[g][:, vsl(h)]) + oi for (g, h), s, oi in zip(chains, sc, o_in)]
    ut = [_mm_tn(vv[g][:, vsl(h)], kdec[g][:, ksl(h)]) for g, h in chains]
    for (g, h), s, u in zip(chains, st, ut):
        st_ref[groups[g][0], groups[g][1], h] = s * dec[g][:, ksl(h)] + u
    for g, (b, d) in enumerate(groups):
        refs[d][3][b] = jnp.concatenate(o[g * GLA_HEADS:(g + 1) * GLA_HEADS], axis=1).astype(refs[d][3].dtype)


def _chunk_specs(nb, width, col, rowf):
    return pl.BlockSpec((nb, CHUNK, width), lambda i: (0, rowf(i), col))


def _lru_kernel(xf_ref, xb_ref, w_ref, gb_ref, lam_ref, of_ref, ob_ref, a_s, b_s, o_s, h_s, *, nb, tl):
    @pl.when(pl.program_id(0) == 0)
    def _():
        h_s[...] = jnp.zeros_like(h_s)

    chains = [(b, d) for b in range(nb) for d in range(2)]
    for n, (b, d) in enumerate(chains):
        xb = (xf_ref, xb_ref)[d][b]
        x = xb.astype(F32)
        gt = _mm(xb, w_ref[d]) + gb_ref[d]
        r = _sigmoid(gt[:, :LRU_WIDTH])
        ig = _sigmoid(gt[:, LRU_WIDTH:])
        log_a = (-LRU_C) * r * _softplus(-lam_ref[d])
        a = jnp.exp(log_a)
        a_s[n] = a
        b_s[n] = jnp.sqrt(1.0 - a * a) * ig * x

    def body(s, hs):
        out = []
        for n, (b, d) in enumerate(chains):
            rr = s if d == 0 else tl - 1 - s
            h = a_s[n, pl.ds(rr, 1), :] * hs[n] + b_s[n, pl.ds(rr, 1), :]
            o_s[n, pl.ds(rr, 1), :] = h
            out.append(h)
        return tuple(out)
    hs = lax.fori_loop(0, tl, body, tuple(h_s[n] for n in range(len(chains))), unroll=8)
    for n, (b, d) in enumerate(chains):
        h_s[n] = hs[n]
        (of_ref, ob_ref)[d][b] = o_s[n].astype(of_ref.dtype)


def _lru_call(p3, wbd, gb, lam, *, nb, tb, ctx_len, tl):
    ntl, nctl = tb // tl, ctx_len // tl
    bwd = lambda i: _bwd_index(i, nctl, ntl)
    out = jax.ShapeDtypeStruct((nb, tb, LRU_WIDTH), BF16)
    const = lambda a: pl.BlockSpec(a.shape, lambda i: (0,) * a.ndim)
    return pl.pallas_call(
        functools.partial(_lru_kernel, nb=nb, tl=tl),
        out_shape=(out, out),
        grid=(ntl,),
        in_specs=[pl.BlockSpec((nb, tl, TN), lambda i: (0, i, T_LRU_X)),
                  pl.BlockSpec((nb, tl, TN), lambda i: (0, bwd(i), T_LRU_X)),
                  const(wbd), const(gb), const(lam)],
        out_specs=(pl.BlockSpec((nb, tl, LRU_WIDTH), lambda i: (0, i, 0)),
                   pl.BlockSpec((nb, tl, LRU_WIDTH), lambda i: (0, bwd(i), 0))),
        scratch_shapes=[pltpu.VMEM((2 * nb, tl, LRU_WIDTH), F32)] * 3 + [pltpu.VMEM((2 * nb, 1, LRU_WIDTH), F32)],
        compiler_params=_cparams(("arbitrary",)),
        name="lru",
    )(p3, p3, wbd, gb, lam)


def _unit_tri_inverses(a_list, row, col):
    eye = jnp.where(row == col, 1.0, 0.0)
    blk = lambda s: (row // s) == (col // s)
    in8 = blk(8)
    a0 = [jnp.where(in8, a, 0.0) for a in a_list]
    p2 = [_mmb(x, x) for x in a0]
    ia = [eye - x for x in a0]
    t = [x + _mmb(x, p) for x, p in zip(ia, p2)]
    p4 = [_mmb(p, p) for p in p2]
    t = [x + _mmb(x, p) for x, p in zip(t, p4)]
    s = 8
    while s < CHUNK:
        sel = blk(2 * s) & jnp.logical_not(blk(s))
        y = [_mmb(jnp.where(sel, a, 0.0), x) for a, x in zip(a_list, t)]
        t = [x - _mmb(x, v) for x, v in zip(t, y)]
        s *= 2
    return t


def _gdn_step(qf_ref, kf_ref, vf_ref, gf_ref, qb_ref, kb_ref, vb_ref, gb_ref, alog_ref, dtb_ref,
              of_ref, ob_ref, s_ref, nb):
    c = CHUNK
    row, col, incl, strict, tri_bf = _tri_masks(c)
    refs = (((qf_ref, kf_ref, vf_ref), gf_ref, of_ref), ((qb_ref, kb_ref, vb_ref), gb_ref, ob_ref))
    neg_a = -jnp.exp(alog_ref[...])
    groups = [(b, d) for b in range(nb) for d in range(2)]
    gblk = [refs[d][1][b] for b, d in groups]
    g_all = [neg_a * _softplus(v + dtb_ref[...]) for v in gblk]
    beta_all = [_sigmoid(v) for v in gblk]
    gc_all = [_cumdot(tri_bf[d], v) for (b, d), v in zip(groups, g_all)]
    gc_t = [v.T for v in gc_all]

    chains = [(g, h) for g in range(len(groups)) for h in range(GDN_HEADS)]
    dirn = lambda g: groups[g][1]
    lane_a = lambda g, h: SM_A + dirn(g) * GDN_HEADS + h
    lane_b = lambda g, h: SM_B + dirn(g) * GDN_HEADS + h
    gcol = [gc_all[g][:, lane_a(g, h):lane_a(g, h) + 1] for g, h in chains]
    grow = [gc_t[g][lane_a(g, h):lane_a(g, h) + 1, :] for g, h in chains]
    beta = [beta_all[g][:, lane_b(g, h):lane_b(g, h) + 1] for g, h in chains]
    glast = [v[c - 1:c, :] if dirn(g) == 0 else v[0:1, :] for (g, h), v in zip(chains, gcol)]
    decay = [jnp.exp(jnp.where(incl[dirn(g)], gc - gr, -jnp.inf)) for (g, h), gc, gr in zip(chains, gcol, grow)]

    def head(g, h, part):
        b, d = groups[g]
        return refs[d][0][part][b, :, h * GDN_DK:(h + 1) * GDN_DK]
    k_bf = [head(g, h, 1) for g, h in chains]
    qh = [head(g, h, 0).astype(F32) for g, h in chains]
    kh = [k.astype(F32) for k in k_bf]
    vh = [head(g, h, 2).astype(F32) for g, h in chains]
    kb = [k * bt for k, bt in zip(kh, beta)]
    kq = [_mm_nt(jnp.concatenate([x.astype(BF16), y.astype(BF16)], axis=0), k) for x, y, k in zip(kb, qh, k_bf)]
    a = [jnp.where(strict[dirn(g)], x[:c] * dc, 0.0) for (g, h), x, dc in zip(chains, kq, decay)]
    attn = [jnp.where(incl[dirn(g)], x[c:] * dc, 0.0).astype(BF16) for (g, h), x, dc in zip(chains, kq, decay)]
    t = _unit_tri_inverses(a, row, col)
    egc = [jnp.exp(v) for v in gcol]
    sol = [_mmb(x, jnp.concatenate([v * bt, k * e], axis=1)) for x, v, bt, k, e in zip(t, vh, beta, kb, egc)]
    kdec = [(k * jnp.exp(gl - gc)).astype(BF16) for k, gl, gc in zip(kh, glast, gcol)]
    qdec = [(x * e).astype(BF16) for x, e in zip(qh, egc)]
    s = [s_ref[groups[g][0], groups[g][1], h] for g, h in chains]
    s_bf = [v.astype(BF16) for v in s]
    ws = [_mm(jnp.concatenate([x[:, GDN_DV:].astype(BF16), qd], axis=0), sb) for x, qd, sb in zip(sol, qdec, s_bf)]
    v_new = [x[:, :GDN_DV] - y[:c] for x, y in zip(sol, ws)]
    vn_bf = [v.astype(BF16) for v in v_new]
    o = [y[c:] + _mm(at, vn) for y, at, vn in zip(ws, attn, vn_bf)]
    for (g, h), sv, gl, kd, vn in zip(chains, s, glast, kdec, vn_bf):
        s_ref[groups[g][0], groups[g][1], h] = sv * jnp.exp(gl) + _mm_tn(kd, vn)
    for g, (b, d) in enumerate(groups):
        refs[d][2][b] = jnp.concatenate(o[g * GDN_HEADS:(g + 1) * GDN_HEADS], axis=1).astype(refs[d][2].dtype)


def _chunk_scan_kernel(gla_qkf, gla_vf, smallf, gla_qkb, gla_vb, smallb, wg_ref, bg_ref,
                       gdn_qf, gdn_kf, gdn_vf, gdn_qb, gdn_kb, gdn_vb, alog_ref, dtb_ref,
                       gla_of, gla_ob, gdn_of, gdn_ob, gla_state, gdn_state, *, nb):
    @pl.when(pl.program_id(0) == 0)
    def _():
        gla_state[...] = jnp.zeros_like(gla_state)
        gdn_state[...] = jnp.zeros_like(gdn_state)

    _gla_step(gla_qkf, gla_vf, smallf, gla_qkb, gla_vb, smallb, wg_ref, bg_ref, gla_of, gla_ob, gla_state, nb)
    _gdn_step(gdn_qf, gdn_kf, gdn_vf, smallf, gdn_qb, gdn_kb, gdn_vb, smallb, alog_ref, dtb_ref,
              gdn_of, gdn_ob, gdn_state, nb)


def _chunk_scan_call(p3, ps3, wg, bg, alog, dtb, *, nb, tb, ctx_len):
    nct, ncc = tb // CHUNK, ctx_len // CHUNK
    fwd = lambda i: i
    bwd = lambda i: _bwd_index(i, ncc, nct)
    width = GLA_HEADS * GLA_DV
    assert width == GDN_HEADS * GDN_DV
    tile = lambda t, rowf: _chunk_specs(nb, TN, t, rowf)
    small = lambda rowf: _chunk_specs(nb, SMALL_W, 0, rowf)
    const = lambda a: pl.BlockSpec(a.shape, lambda i: (0,) * a.ndim)
    gla_in = lambda rowf: [tile(T_GLA_QK, rowf), tile(T_GLA_V, rowf), small(rowf)]
    gdn_in = lambda rowf: [tile(t, rowf) for t in (T_GDN_Q, T_GDN_K, T_GDN_V)]
    out = jax.ShapeDtypeStruct((nb, tb, width), BF16)
    out_spec = lambda rowf: _chunk_specs(nb, width, 0, rowf)
    return pl.pallas_call(
        functools.partial(_chunk_scan_kernel, nb=nb),
        out_shape=(out, out, out, out),
        grid=(nct,),
        in_specs=gla_in(fwd) + gla_in(bwd) + [const(wg), const(bg)] + gdn_in(fwd) + gdn_in(bwd)
        + [const(alog), const(dtb)],
        out_specs=(out_spec(fwd), out_spec(bwd), out_spec(fwd), out_spec(bwd)),
        scratch_shapes=[pltpu.VMEM((nb, 2, GLA_HEADS, GLA_DV, GLA_DK), F32),
                        pltpu.VMEM((nb, 2, GDN_HEADS, GDN_DK, GDN_DV), F32)],
        compiler_params=_cparams(("arbitrary",)),
        name="chunk_scan",
    )(p3, p3, ps3, p3, p3, ps3, wg, bg, p3, p3, p3, p3, p3, p3, alog, dtb)


def _na_kernel(q_ref, k0_ref, k1_ref, k2_ref, v0_ref, v1_ref, v2_ref, kc_ref, vc_ref, bias_ref, o_ref,
               kbuf, vbuf, vcbuf, *, rows):
    jj = pl.program_id(1)
    tq = NA_ROWS * GRID_W
    hd = NA_HD

    @pl.when(jj == 0)
    def _():
        outs = []
        for h in range(NA_HEADS):
            sl = slice(h * hd, (h + 1) * hd)
            s = _mm_nt(q_ref[:, sl], k1_ref[:, sl])
            m = jnp.max(s, axis=-1, keepdims=True)
            e = jnp.exp(s - m)
            l = jnp.sum(e, axis=-1, keepdims=True)
            outs.append(_mm(e.astype(BF16), v1_ref[:, sl]) / l)
        o_ref[...] = jnp.concatenate(outs, axis=1).astype(o_ref.dtype)

    @pl.when(jj > 0)
    def _():
        jp = jj - 1
        pw = 2 * hd
        pairs = range(NA_HEADS // 2)
        ones = jnp.ones((tq, pw), BF16)
        for n, (kr, vr) in enumerate(((k0_ref, v0_ref), (k1_ref, v1_ref), (k2_ref, v2_ref))):
            kbuf[n * tq:(n + 1) * tq, :] = kr[...]
            for pr in pairs:
                vbuf[n * tq:(n + 1) * tq, 2 * pr * pw:(2 * pr + 1) * pw] = vr[:, pr * pw:(pr + 1) * pw]
                vbuf[n * tq:(n + 1) * tq, (2 * pr + 1) * pw:(2 * pr + 2) * pw] = ones
        for pr in pairs:
            vcbuf[:, 2 * pr * pw:(2 * pr + 1) * pw] = vc_ref[:, pr * pw:(pr + 1) * pw]
            vcbuf[:, (2 * pr + 1) * pw:(2 * pr + 2) * pw] = ones[:vcbuf.shape[0]]
        first = lax.broadcasted_iota(jnp.int32, (1, pw), 1) < hd
        for a in range(NA_ROWS):
            r = NA_ROWS * jp + a
            rs = jnp.clip(r - NA_WIN_R // 2, 0, rows - NA_WIN_R)
            start = pl.multiple_of((rs - NA_ROWS * (jp - 1)) * GRID_W, GRID_W)
            e_idx = r - rs
            win = pl.ds(start, NA_WIN_R * GRID_W)
            qp = [q_ref[a * GRID_W:(a + 1) * GRID_W, pr * pw:(pr + 1) * pw] for pr in pairs]
            zero = jnp.zeros_like(qp[0])
            q2 = [jnp.concatenate([jnp.where(first, x, zero), jnp.where(first, zero, x)], axis=0) for x in qp]
            s_loc = [_mm_nt(q2[pr], kbuf[win, pr * pw:(pr + 1) * pw]) + bias_ref[e_idx, pr] for pr in pairs]
            s_ctx = [_mm_nt(q2[pr], kc_ref[:, pr * pw:(pr + 1) * pw]) for pr in pairs]
            nk = s_ctx[0].shape[1]
            m = []
            for pr in pairs:
                mx = s_ctx[pr]
                for k0 in range(0, s_loc[pr].shape[1], nk):
                    mx = jnp.maximum(mx, s_loc[pr][:, k0:k0 + nk])
                m.append(jnp.max(mx, axis=-1, keepdims=True))
            p_loc = [jnp.exp(s_loc[pr] - m[pr]).astype(BF16) for pr in pairs]
            p_ctx = [jnp.exp(s_ctx[pr] - m[pr]).astype(BF16) for pr in pairs]
            oe = [_mm(p_loc[pr], vbuf[win, 2 * pr * pw:(2 * pr + 2) * pw])
                  + _mm(p_ctx[pr], vcbuf[:, 2 * pr * pw:(2 * pr + 2) * pw]) for pr in pairs]
            on = [x[:, :pw] / x[:, pw:] for x in oe]
            o = [jnp.where(first, x[:GRID_W], x[GRID_W:]) for x in on]
            o_ref[a * GRID_W:(a + 1) * GRID_W, :] = jnp.concatenate(o, axis=1).astype(o_ref.dtype)


def _na_call(p, bias, *, nb, tb, ctx_len):
    r = p.shape[0]
    tq = NA_ROWS * GRID_W
    assert ctx_len == tq, "context block must be exactly one query block"
    nblk = tb // tq
    nlb = nblk - 1
    rows = (tb - ctx_len) // GRID_W
    assert rows >= NA_WIN_R and rows % NA_ROWS == 0
    width = NA_HEADS * NA_HD
    cur = lambda b, j: b * nblk + j
    prv = lambda b, j: b * nblk + 1 + jnp.clip(j - 2, 0, nlb - 1)
    nxt = lambda b, j: b * nblk + 1 + jnp.clip(j, 0, nlb - 1)
    blk = lambda rowf, t: pl.BlockSpec((tq, TN), lambda b, j: (rowf(b, j), t))
    return pl.pallas_call(
        functools.partial(_na_kernel, rows=rows),
        out_shape=jax.ShapeDtypeStruct((r, width), BF16),
        grid=(nb, nblk),
        in_specs=[blk(cur, T_NA_Q),
                  blk(prv, T_NA_K), blk(cur, T_NA_K), blk(nxt, T_NA_K),
                  blk(prv, T_NA_V), blk(cur, T_NA_V), blk(nxt, T_NA_V),
                  pl.BlockSpec((ctx_len, TN), lambda b, j: (b * nblk, T_NA_K)),
                  pl.BlockSpec((ctx_len, TN), lambda b, j: (b * nblk, T_NA_V)),
                  pl.BlockSpec(bias.shape, lambda b, j: (0, 0, 0, 0), pipeline_mode=pl.Buffered(1))],
        out_specs=pl.BlockSpec((tq, width), lambda b, j: (cur(b, j), 0)),
        scratch_shapes=[pltpu.VMEM((3 * tq, TN), BF16), pltpu.VMEM((3 * tq, 2 * TN), BF16),
                        pltpu.VMEM((ctx_len, 2 * TN), BF16)],
        compiler_params=_cparams(("arbitrary", "arbitrary")),
        name="na",
    )(p, p, p, p, p, p, p, p, p, bias)


def _na_bias_table(rpb):
    c = np.arange(GRID_W)
    cs = np.clip(c - NA_WIN_C // 2, 0, GRID_W - NA_WIN_C)
    kc = np.arange(GRID_W)
    inwin = (kc[None, :] >= cs[:, None]) & (kc[None, :] < cs[:, None] + NA_WIN_C)
    dc = np.clip(kc[None, :] - c[:, None] + NA_WIN_C - 1, 0, 2 * NA_WIN_C - 2)
    onehot = np.zeros((2 * NA_WIN_C - 1, GRID_W * GRID_W), np.float32)
    onehot[dc.reshape(-1), np.arange(GRID_W * GRID_W)] = 1.0
    byrow = jnp.einsum('hrd,dm->hrm', rpb, onehot, precision=lax.Precision.HIGHEST)
    byrow = jnp.where(inwin[None, None], byrow.reshape(NA_HEADS, 2 * NA_WIN_R - 1, GRID_W, GRID_W), NEG_BIG)
    tabs = [jnp.transpose(byrow[:, NA_WIN_R - 1 - e:2 * NA_WIN_R - 1 - e], (0, 2, 1, 3))
            .reshape(NA_HEADS, GRID_W, NA_WIN_R * GRID_W) for e in range(NA_WIN_R)]
    return jnp.stack(tabs).reshape(NA_WIN_R, NA_HEADS // 2, 2 * GRID_W, NA_WIN_R * GRID_W)


def _head_rms(o, g, width):
    outs = []
    for h in range(o.shape[1] // width):
        xs = o[:, h * width:(h + 1) * width]
        ms = jnp.mean(xs * xs, axis=-1, keepdims=True)
        outs.append(xs * lax.rsqrt(ms + NORM_EPS) * g)
    return jnp.concatenate(outs, axis=1)


def _merge_kernel(gof_ref, gob_ref, gg_ref, lhf_ref, lhb_ref, ly_ref, dof_ref, dob_ref, dz_ref, na_ref,
                  m0_ref, m1_ref, m2_ref, m3_ref, x_ref, g1l_ref, g1c_ref, wbr_ref, wout_ref, bout_ref,
                  gng_ref, gnd_ref, lng_ref, lnb_ref, o_ref, *, tm, tpb, ctx_len, alpha):
    p0 = (pl.program_id(0) % tpb) * tm
    for s0, ns in _sub_blocks(tm):
        rows = slice(s0, s0 + ns)
        f = lambda ref: ref[rows, :].astype(F32)
        y_a = _head_rms(f(gof_ref) + f(gob_ref), gng_ref[...], GLA_DV) * _silu(f(gg_ref))
        y_b = (f(lhf_ref) + f(lhb_ref)) * _gelu_tanh(f(ly_ref))
        y_c = _head_rms(f(dof_ref) + f(dob_ref), gnd_ref[...], GDN_DV) * _silu(f(dz_ref))
        y_d = na_ref[rows, :]
        m = None
        for n, (y, g_ref) in enumerate(((y_a, m0_ref), (y_b, m1_ref), (y_c, m2_ref), (y_d, m3_ref))):
            term = f(g_ref) * _mm(y.astype(BF16), wbr_ref[n])
            m = term if m is None else m + term
        out = _mm(m.astype(BF16), wout_ref[...]) + bout_ref[...]
        is_ctx = _row_pos(ns, p0 + s0) < ctx_len
        g1 = jnp.where(is_ctx, g1c_ref[...], g1l_ref[0])
        o_ref[rows, :] = _layer_norm(alpha * x_ref[rows, :] + g1 * out, lng_ref[...], lnb_ref[...])


def _merge_call(p, gof, gob, lhf, lhb, dof, dob, na, xa, g1l, g1c, wbr, wout, bout, gng, gnd, lng, lnb,
                *, tb, ctx_len, tm, alpha):
    r, d = xa.shape
    tpb = tb // tm
    bw = BRANCH_W
    row = lambda w: pl.BlockSpec((tm, w), lambda i: (i, 0))
    ptile = lambda t, w: pl.BlockSpec((tm, w), lambda i: (i, t * TN // w))
    const = lambda a: pl.BlockSpec(a.shape, lambda i: (0,) * a.ndim)
    kern = functools.partial(_merge_kernel, tm=tm, tpb=tpb, ctx_len=ctx_len, alpha=alpha)
    return pl.pallas_call(
        kern,
        out_shape=jax.ShapeDtypeStruct((r, d), F32),
        grid=(r // tm,),
        in_specs=[row(bw), row(bw), ptile(T_GLA_G, bw),
                  row(bw), row(bw),
                  ptile(T_LRU_Y, bw),
                  row(bw), row(bw), ptile(T_GDN_Z, bw),
                  row(bw),
                  pl.BlockSpec((tm, d), lambda i: (i, T_MERGE * TN // d + 0)),
                  pl.BlockSpec((tm, d), lambda i: (i, T_MERGE * TN // d + 1)),
                  pl.BlockSpec((tm, d), lambda i: (i, T_MERGE * TN // d + 2)),
                  pl.BlockSpec((tm, d), lambda i: (i, T_MERGE * TN // d + 3)),
                  row(d),
                  pl.BlockSpec((1, 1, d), lambda i: (i // tpb, 0, 0)),
                  const(g1c), const(wbr), const(wout), const(bout), const(gng), const(gnd), const(lng), const(lnb)],
        out_specs=row(d),
        compiler_params=_cparams(("arbitrary",)),
        name="merge",
    )(gof, gob, p, lhf, lhb, p, dof, dob, p, na, p, p, p, p, xa, g1l, g1c, wbr, wout, bout, gng, gnd, lng, lnb)


def _ffn_kernel(x_ref, xp_ref, xn_ref, scl_ref, shl_ref, gl_ref, scc_ref, shc_ref, gc_ref, wa_ref, wb_ref,
                ba_ref, bb_ref, cw_ref, cb_ref, wd_ref, bd_ref, lng_ref, lnb_ref, o_ref,
                lhs_ref, a_ref, h_ref, *, tm, tpb, ctx_len, tb, nc, tc, alpha):
    i = pl.program_id(0)
    c = pl.program_id(1)
    p0 = (i % tpb) * tm

    @pl.when(c == 0)
    def _():
        _build_lhs(lhs_ref, x_ref, xp_ref, xn_ref, scl_ref[0], shl_ref[0], scc_ref[...], shc_ref[...],
                   p0, tm, ctx_len)

    left = FFN_CONV // 2
    subs = _sub_blocks(tm)
    for s0, n in subs:
        a0 = _acc_row(s0, tm)
        a_ref[a0 - HALO:a0 + n + HALO, :] = _mm(lhs_ref[s0:s0 + n + 2 * HALO, :], wa_ref[...]) + ba_ref[...]
    gates = [_mm(lhs_ref[HALO + s0:HALO + s0 + n, :], wb_ref[...]) + bb_ref[...] for s0, n in subs]
    cw = cw_ref[...]
    hidden = [(_silu(_dwconv(a_ref, cw, left, _acc_row(s0, tm), n) + cb_ref[...]) * g).astype(BF16)
              for (s0, n), g in zip(subs, gates)]

    def gate_rows(r0):
        n = subs[0][1]
        return gates[r0 // n][r0 % n:r0 % n + HALO, :]
    fixes = [(r0, (_silu(_dwconv(a_ref, cw, left, _acc_row(r0, tm), HALO, p0 + r0, ctx_len, tb) + cb_ref[...])
                   * gate_rows(r0)).astype(BF16)) for r0 in _segment_end_windows(tm, ctx_len)]
    for cc in range(nc):
        @pl.when(c == cc)
        def _(cc=cc):
            for (s0, n), h in zip(subs, hidden):
                h_ref[s0:s0 + n, cc * tc:(cc + 1) * tc] = h
            for r0, fixed in fixes:
                h_ref[r0:r0 + HALO, cc * tc:(cc + 1) * tc] = fixed

    @pl.when(c == nc - 1)
    def _():
        for s0, n in subs:
            f = _mm(h_ref[s0:s0 + n, :], wd_ref[...]) + bd_ref[...]
            is_ctx = _row_pos(n, p0 + s0) < ctx_len
            g2 = jnp.where(is_ctx, gc_ref[...], gl_ref[0])
            o_ref[s0:s0 + n, :] = _layer_norm(alpha * x_ref[s0:s0 + n, :] + g2 * f, lng_ref[...], lnb_ref[...])


def _ffn_call(xa, scl, shl, gl, scc, shc, gc, w_up, b_up, cw, cb, w_down, b_down, lng, lnb,
              *, tb, ctx_len, tm, tc, alpha):
    r, d = xa.shape
    dff = w_down.shape[0]
    nc = dff // tc
    tpb = tb // tm
    lat = pl.BlockSpec((1, 1, d), lambda i, c: (i // tpb, 0, 0))
    vec = pl.BlockSpec((1, d), lambda i, c: (0, 0))
    kern = functools.partial(_ffn_kernel, tm=tm, tpb=tpb, ctx_len=ctx_len, tb=tb, nc=nc, tc=tc, alpha=alpha)
    return pl.pallas_call(
        kern,
        out_shape=jax.ShapeDtypeStruct((r, d), F32),
        grid=(r // tm, nc),
        in_specs=_halo_specs(tm, d, r) + [
            lat, lat, lat, vec, vec, vec,
            pl.BlockSpec((d, tc), lambda i, c: (0, c)),
            pl.BlockSpec((d, tc), lambda i, c: (0, nc + c)),
            pl.BlockSpec((1, tc), lambda i, c: (0, c)),
            pl.BlockSpec((1, tc), lambda i, c: (0, nc + c)),
            pl.BlockSpec((FFN_CONV, tc), lambda i, c: (0, c)),
            pl.BlockSpec((1, tc), lambda i, c: (0, c)),
            pl.BlockSpec((dff, d), lambda i, c: (0, 0), pipeline_mode=pl.Buffered(1)),
            vec, vec, vec],
        out_specs=pl.BlockSpec((tm, d), lambda i, c: (i, 0)),
        scratch_shapes=[pltpu.VMEM((tm + 2 * HALO, d), BF16),
                        pltpu.VMEM((tm + 2 * HALO * len(_sub_blocks(tm)), tc), F32),
                        pltpu.VMEM((tm, dff), BF16)],
        compiler_params=_cparams(("arbitrary", "arbitrary")),
        name="ffn",
    )(xa, xa, xa, scl, shl, gl, scc, shc, gc, w_up, w_up, b_up, b_up, cw, cb, w_down, b_down, lng, lnb)


def _pad_cols(a, width):
    return jnp.pad(a, ((0, 0), (0, width - a.shape[1])))


def _in_proj_params(w_in, b_in, lru_conv_w, lru_conv_b, gdn_conv_w):
    sizes = (GLA_HEADS * GLA_DK, GLA_HEADS * GLA_DK, GLA_HEADS * GLA_DV, GLA_HEADS * GLA_DV,
             GLA_GATE_RANK, GLA_GATE_RANK, LRU_WIDTH, LRU_WIDTH,
             GDN_HEADS * GDN_DK, GDN_HEADS * GDN_DK, GDN_HEADS * GDN_DV, GDN_HEADS * GDN_DV,
             2 * GDN_HEADS, 2 * GDN_HEADS, NA_HEADS * NA_HD, NA_HEADS * NA_HD, NA_HEADS * NA_HD)
    names = ('gla_q', 'gla_k', 'gla_v', 'gla_g', 'gla_fw', 'gla_bw', 'lru_x', 'lru_y', 'gdn_q', 'gdn_k',
             'gdn_v', 'gdn_z', 'gdn_a', 'gdn_b', 'na_q', 'na_k', 'na_v')
    offs = np.concatenate([[0], np.cumsum(sizes)])

    def reorder(m):
        col = {n: m[:, offs[k]:offs[k + 1]] for k, n in enumerate(names)}
        col['merge'] = m[:, offs[-1]:]
        small = _pad_cols(jnp.concatenate([col['gla_fw'], col['gla_bw'], col['gdn_a'], col['gdn_b']], axis=1), TN)
        tiles = {T_GLA_QK: jnp.concatenate([col['gla_q'] * (GLA_DK ** -0.5), col['gla_k']], axis=1),
                 T_GLA_V: col['gla_v'], T_GLA_G: col['gla_g'], T_GDN_Q: col['gdn_q'], T_GDN_K: col['gdn_k'],
                 T_GDN_V: col['gdn_v'], T_GDN_Z: col['gdn_z'], T_LRU_X: col['lru_x'], T_LRU_Y: col['lru_y'],
                 T_NA_Q: col['na_q'] * (NA_HD ** -0.5), T_NA_K: col['na_k'], T_NA_V: col['na_v'],
                 T_SMALL: _pad_cols(small, 2 * TN)}
        order = [tiles[t] for t in range(T_MERGE)] + [col['merge'], tiles[T_SMALL]]
        return jnp.concatenate(order, axis=1)
    w = reorder(w_in.astype(BF16))
    b = reorder(b_in[None, :])
    n = w.shape[1]
    assert n == N_TILES * TN
    cw = jnp.zeros((LRU_CONV, n), F32)
    cw = cw.at[:, T_LRU_X * TN:(T_LRU_X + 1) * TN].set(lru_conv_w)
    for k, t in enumerate((T_GDN_Q, T_GDN_K, T_GDN_V)):
        cw = cw.at[:, t * TN:(t + 1) * TN].set(gdn_conv_w[:, k * TN:(k + 1) * TN])
    cb = jnp.zeros((1, n), F32).at[0, T_LRU_X * TN:(T_LRU_X + 1) * TN].set(lru_conv_b)
    return w, b, cw, cb


def _rope_tables(ctx_len, seq_len):
    quarter = GLA_DK // 4
    t = jnp.arange(seq_len)
    inv = ROPE_BASE ** (-jnp.arange(quarter, dtype=F32) / quarter)
    ang = jnp.concatenate([(t // GRID_W).astype(F32)[:, None] * inv,
                           (t % GRID_W).astype(F32)[:, None] * inv], -1)
    cos = jnp.concatenate([jnp.ones((ctx_len, 2 * quarter), F32), jnp.cos(ang)], axis=0)
    sin = jnp.concatenate([jnp.zeros((ctx_len, 2 * quarter), F32), jnp.sin(ang)], axis=0)
    return jnp.concatenate([cos, cos, cos, cos], axis=1), jnp.concatenate([-sin, sin, -sin, sin], axis=1)


def _lru_gate_matrix(gate_w):
    eye = jnp.eye(LRU_BLOCKS, dtype=F32)
    dense = jnp.einsum('xgnde,nm->xndgme', gate_w, eye)
    return dense.reshape(2, LRU_WIDTH, 2 * LRU_WIDTH).astype(BF16)


def _small_lane_vec(v):
    return jnp.zeros((1, 128), F32).at[0, SM_A:SM_A + 2 * GDN_HEADS].set(v.reshape(-1))


def _row_tile(tb, target):
    best = HALO
    for t in range(HALO, target + 1, HALO):
        if tb % t == 0:
            best = t
    return best


def kernel(x, c, ctx, c_ctx, w_mod, b_mod, w_in, b_in, gla_w_gate, gla_b_gate, gla_norm, lru_conv_w, lru_conv_b,
           lru_gate_w, lru_gate_b, lru_lambda, gdn_conv_w, gdn_a_log, gdn_dt_bias, gdn_norm, na_rpb, w_branch,
           w_out, b_out, ln1_g, ln1_b, ffn_w_up, ffn_b_up, ffn_conv_w, ffn_conv_b, ffn_w_down, ffn_b_down,
           ln2_g, ln2_b):
    nb, seq_len, d = x.shape
    ctx_len = ctx.shape[1]
    depth = w_mod.shape[0]
    tb = ctx_len + seq_len
    alpha = (2 * depth) ** 0.25
    tm = _row_tile(tb, ROW_TILE)
    tl = NA_ROWS * GRID_W
    dims = dict(nb=nb, tb=tb, ctx_len=ctx_len)

    xa = jnp.concatenate([ctx, x], axis=1).reshape(nb * tb, d)
    crows = jnp.zeros((8, d), F32).at[:nb].set(c).at[nb].set(c_ctx)
    mods = _mod_call(crows, w_mod, b_mod)
    cos, sin = _rope_tables(ctx_len, seq_len)

    for l in range(depth):
        lat = [mods[l, :nb, k * d:(k + 1) * d].reshape(nb, 1, d) for k in range(6)]
        cx = [mods[l, nb:nb + 1, k * d:(k + 1) * d] for k in range(6)]
        w, b, cw, cb = _in_proj_params(w_in[l], b_in[l], lru_conv_w[l], lru_conv_b[l], gdn_conv_w[l])
        p, ps = _proj_call(xa, lat[1], lat[0], cx[1], cx[0], w, b, cw, cb, cos, sin, tm=tm, **dims)

        p3, ps3 = p.reshape(nb, tb, -1), ps.reshape(nb, tb, -1)
        flat = lambda a: a.reshape(nb * tb, -1)
        gof, gob, dof, dob = map(flat, _chunk_scan_call(
            p3, ps3, gla_w_gate[l], gla_b_gate[l].reshape(2, 1, -1),
            _small_lane_vec(gdn_a_log[l]), _small_lane_vec(gdn_dt_bias[l]), **dims))
        lhf, lhb = map(flat, _lru_call(p3, _lru_gate_matrix(lru_gate_w[l]), lru_gate_b[l].reshape(2, 1, -1),
                                       lru_lambda[l].reshape(2, 1, -1), tl=tl, **dims))
        na = _na_call(p, _na_bias_table(na_rpb[l]), **dims)

        xa = _merge_call(p, gof, gob, lhf, lhb, dof, dob, na, xa, lat[2], cx[2],
                         w_branch[l].astype(BF16), w_out[l].astype(BF16), b_out[l].reshape(1, d),
                         gla_norm[l].reshape(1, -1),
                         gdn_norm[l].reshape(1, -1), ln1_g[l].reshape(1, d), ln1_b[l].reshape(1, d),
                         tb=tb, ctx_len=ctx_len, tm=_row_tile(tb, MERGE_TILE), alpha=alpha)
        xa = _ffn_call(xa, lat[4], lat[3], lat[5], cx[4], cx[3], cx[5],
                       ffn_w_up[l].astype(BF16), ffn_b_up[l].reshape(1, -1), ffn_conv_w[l],
                       ffn_conv_b[l].reshape(1, -1), ffn_w_down[l].astype(BF16), ffn_b_down[l].reshape(1, d),
                       ln2_g[l].reshape(1, d), ln2_b[l].reshape(1, d),
                       tb=tb, ctx_len=ctx_len, tm=tm, tc=256, alpha=alpha)
    return xa.reshape(nb, tb, d)[:, ctx_len:, :]
```

```python
import functools
import math

import jax
import jax.numpy as jnp
import numpy as np
from jax import lax
from jax.experimental import pallas as pl
from jax.experimental.pallas import tpu as pltpu

F32 = jnp.float32
BF16 = jnp.bfloat16

GRID_W = 64
CHUNK = 64
GLA_HEADS, GLA_DK, GLA_DV = 4, 64, 128
GLA_GATE_RANK = 16
GLA_GATE_NORM = 16.0
ROPE_BASE = 10000.0
LRU_WIDTH, LRU_BLOCKS, LRU_CONV, LRU_C = 512, 8, 4, 8.0
GDN_HEADS, GDN_DK, GDN_DV, GDN_CONV = 4, 128, 128, 4
NA_HEADS, NA_HD, NA_WIN_R, NA_WIN_C = 8, 64, 8, 16
FFN_CONV = 3
LN_EPS = 1e-5
NORM_EPS = 1e-6
BRANCH_W = 512

HALO = 16
ROW_TILE = 1280
MERGE_TILE = 512
ROW_SUB = 4
TN = 512
NA_ROWS = 4
VMEM_LIMIT = 56 * 1024 * 1024
NEG_BIG = -1e30

T_GLA_QK, T_GLA_V = 0, 1
T_GDN_Q, T_GLA_G = 2, 3
T_GDN_K, T_GDN_Z = 4, 5
T_GDN_V, T_LRU_Y = 6, 7
T_LRU_X, T_NA_Q = 8, 9
T_NA_K, T_NA_V = 10, 11
T_MERGE = 12
T_SMALL = 20
N_TILES = 22
SMALL_W = 128
SM_FW, SM_BW, SM_A, SM_B = 0, 16, 32, 40


def _cparams(sem):
    return pltpu.CompilerParams(dimension_semantics=sem, vmem_limit_bytes=VMEM_LIMIT)


def _mm(a, b):
    return jnp.dot(a, b, preferred_element_type=F32)


def _mm_nt(a, b):
    return lax.dot_general(a, b, (((1,), (1,)), ((), ())), preferred_element_type=F32)


def _mm_tn(a, b):
    return lax.dot_general(a, b, (((0,), (0,)), ((), ())), preferred_element_type=F32)


def _mmb(a, b):
    return _mm(a.astype(BF16), b.astype(BF16))


def _cumdot(tri_bf, x):
    hi = x.astype(BF16)
    lo = (x - hi.astype(F32)).astype(BF16)
    return _mm(tri_bf, hi) + _mm(tri_bf, lo)


def _sigmoid(x):
    return 0.5 * jnp.tanh(0.5 * x) + 0.5


def _silu(x):
    return x * _sigmoid(x)


def _softplus(x):
    return jnp.maximum(x, 0.0) + jnp.log1p(jnp.exp(-jnp.abs(x)))


def _gelu_tanh(x):
    return x * (0.5 * (1.0 + jnp.tanh(math.sqrt(2.0 / math.pi) * (x + 0.044715 * (x * x * x)))))


def _layer_norm(y, g, b):
    mu = jnp.mean(y, axis=-1, keepdims=True)
    yc = y - mu
    var = jnp.mean(yc * yc, axis=-1, keepdims=True)
    return yc * lax.rsqrt(var + LN_EPS) * g + b


def _row_pos(n, off):
    return lax.broadcasted_iota(jnp.int32, (n, 1), 0) + off


def _mod_kernel(c_ref, w_ref, b_ref, o_ref):
    cv = c_ref[...]
    o_ref[0] = jnp.dot(_silu(cv), w_ref[0], precision=lax.Precision.HIGHEST,
                       preferred_element_type=F32) + b_ref[0]


def _mod_call(crows, w_mod, b_mod):
    depth, d, n = w_mod.shape
    tn = n // 4
    return pl.pallas_call(
        _mod_kernel,
        out_shape=jax.ShapeDtypeStruct((depth, 8, n), F32),
        grid=(depth, n // tn),
        in_specs=[pl.BlockSpec((8, d), lambda l, j: (0, 0)),
                  pl.BlockSpec((1, d, tn), lambda l, j: (l, 0, j)),
                  pl.BlockSpec((1, 1, tn), lambda l, j: (l, 0, j))],
        out_specs=pl.BlockSpec((1, 8, tn), lambda l, j: (l, 0, j)),
        compiler_params=_cparams(("arbitrary", "arbitrary")),
        name="mod",
    )(crows, w_mod, b_mod.reshape(depth, 1, n))


def _build_lhs(lhs_ref, x_ref, xp_ref, xn_ref, scl, shl, scc, shc, p0, tm, ctx_len):
    def mod(xv, p):
        is_ctx = (p >= 0) & (p < ctx_len)
        sc = jnp.where(is_ctx, scc, scl)
        sh = jnp.where(is_ctx, shc, shl)
        return (xv * (1.0 + sc) + sh).astype(BF16)
    lhs_ref[0:HALO, :] = mod(xp_ref[...], _row_pos(HALO, p0 - HALO))
    lhs_ref[HALO:HALO + tm, :] = mod(x_ref[...], _row_pos(tm, p0))
    lhs_ref[HALO + tm:, :] = mod(xn_ref[...], _row_pos(HALO, p0 + tm))


def _dwconv(acc_ref, cw, left, a0, n, pos=None, ctx_len=None, tb=None):
    if pos is not None:
        p = _row_pos(n, pos)
        in_ctx = p < ctx_len
        seg_lo = jnp.where(in_ctx, 0, ctx_len)
        seg_hi = jnp.where(in_ctx, ctx_len, tb)
    out = None
    if pos is None and a0 % 8 == 0 and n % 8 == 0:
        blk = acc_ref[pl.ds(a0 - HALO, n + 2 * HALO), :]
        for k in range(cw.shape[0]):
            d = k - left
            v = blk if d == 0 else pltpu.roll(blk, (-d) % blk.shape[0], 0)
            term = v[HALO:HALO + n, :] * cw[k:k + 1, :]
            out = term if out is None else out + term
        return out
    for k in range(cw.shape[0]):
        d = k - left
        v = acc_ref[pl.ds(a0 + d, n), :]
        if pos is not None:
            v = jnp.where((p + d >= seg_lo) & (p + d < seg_hi), v, 0.0)
        term = v * cw[k:k + 1, :]
        out = term if out is None else out + term
    return out


def _sub_blocks(tm):
    n = tm // ROW_SUB if tm % (ROW_SUB * HALO) == 0 else tm
    return [(s0, n) for s0 in range(0, tm, n)]


def _acc_row(r, tm):
    n = _sub_blocks(tm)[0][1]
    return (r // n) * (n + 2 * HALO) + HALO + r % n


def _segment_end_windows(tm, ctx_len):
    starts = {0, tm - HALO}
    if ctx_len % tm:
        starts |= {ctx_len % tm - HALO, ctx_len % tm}
    return sorted(starts)


def _halo_specs(tm, d, r):
    hb = tm // HALO
    return [pl.BlockSpec((tm, d), lambda i, j: (i, 0)),
            pl.BlockSpec((HALO, d), lambda i, j: (jnp.maximum(i * hb - 1, 0), 0)),
            pl.BlockSpec((HALO, d), lambda i, j: (jnp.minimum((i + 1) * hb, r // HALO - 1), 0))]


def _swap_half_heads(x):
    n = x.shape[1]
    lane = lax.broadcasted_iota(jnp.int32, x.shape, 1)
    return jnp.where((lane % 64) < 32, pltpu.roll(x, n - 32, 1), pltpu.roll(x, 32, 1))


def _l2_heads(x, width, scale):
    outs = []
    for h in range(x.shape[1] // width):
        xs = x[:, h * width:(h + 1) * width]
        ss = jnp.sum(xs * xs, axis=-1, keepdims=True)
        y = xs * lax.rsqrt(ss + NORM_EPS)
        outs.append(y * scale if scale != 1.0 else y)
    return jnp.concatenate(outs, axis=1)


def _proj_kernel(x_ref, xp_ref, xn_ref, scl_ref, shl_ref, scc_ref, shc_ref, w_ref, b_ref, cw_ref, cb_ref,
                 cos_ref, sin_ref, o_ref, os_ref, lhs_ref, acc_ref, *, tm, tpb, ctx_len, tb):
    i = pl.program_id(0)
    j = pl.program_id(1)
    p0 = (i % tpb) * tm

    @pl.when(j == 0)
    def _():
        _build_lhs(lhs_ref, x_ref, xp_ref, xn_ref, scl_ref[0], shl_ref[0], scc_ref[...], shc_ref[...],
                   p0, tm, ctx_len)

    def col(c):
        return slice(c * TN, (c + 1) * TN)

    def plain_then(c, epilogue, out_ref=o_ref):
        for s0, n in _sub_blocks(tm):
            a = _mm(lhs_ref[HALO + s0:HALO + s0 + n, :], w_ref[:, col(c)]) + b_ref[:, col(c)]
            out_ref[s0:s0 + n, col(c)] = epilogue(a, s0, n).astype(out_ref.dtype)

    def conv_then(c, epilogue):
        left = LRU_CONV // 2
        cw = cw_ref[:, col(c)]
        cb = cb_ref[:, col(c)]
        for s0, n in _sub_blocks(tm):
            a0 = _acc_row(s0, tm)
            acc_ref[a0 - HALO:a0 + n + HALO, :] = (_mm(lhs_ref[s0:s0 + n + 2 * HALO, :], w_ref[:, col(c)])
                                                   + b_ref[:, col(c)])
        for s0, n in _sub_blocks(tm):
            conv = _dwconv(acc_ref, cw, left, _acc_row(s0, tm), n) + cb
            o_ref[s0:s0 + n, col(c)] = epilogue(conv).astype(o_ref.dtype)
        for r0 in _segment_end_windows(tm, ctx_len):
            fixed = _dwconv(acc_ref, cw, left, _acc_row(r0, tm), HALO, p0 + r0, ctx_len, tb) + cb
            o_ref[r0:r0 + HALO, col(c)] = epilogue(fixed).astype(o_ref.dtype)

    def rope(a, s0, n):
        reps = a.shape[1] // cos_ref.shape[1]
        cos = jnp.concatenate([cos_ref[s0:s0 + n, :]] * reps, axis=1)
        sin = jnp.concatenate([sin_ref[s0:s0 + n, :]] * reps, axis=1)
        return a * cos + _swap_half_heads(a) * sin

    ident = lambda a, s0, n: a
    conv_epilogues = {T_GDN_Q: lambda v: _l2_heads(_silu(v), GDN_DK, GDN_DK ** -0.5),
                      T_GDN_K: lambda v: _l2_heads(_silu(v), GDN_DK, 1.0),
                      T_GDN_V: _silu,
                      T_LRU_X: lambda v: v}
    for step in range(T_MERGE // 2):
        @pl.when(j == step)
        def _(step=step):
            for c, t in enumerate((2 * step, 2 * step + 1)):
                if t in conv_epilogues:
                    conv_then(c, conv_epilogues[t])
                else:
                    plain_then(c, rope if t == T_GLA_QK else ident)

    @pl.when((j >= T_MERGE // 2) & (j < T_SMALL // 2))
    def _():
        for c in range(2):
            plain_then(c, lambda a, s0, n: _sigmoid(a))

    @pl.when(j == T_SMALL // 2)
    def _():
        for s0, n in _sub_blocks(tm):
            os_ref[s0:s0 + n, :] = (_mm(lhs_ref[HALO + s0:HALO + s0 + n, :], w_ref[:, :SMALL_W])
                                    + b_ref[:, :SMALL_W])


def _proj_call(xa, scl, shl, scc, shc, w, b, cw, cb, cos, sin, *, nb, tb, ctx_len, tm):
    r, d = xa.shape
    n = w.shape[1]
    tpb = tb // tm
    kern = functools.partial(_proj_kernel, tm=tm, tpb=tpb, ctx_len=ctx_len, tb=tb)
    return pl.pallas_call(
        kern,
        out_shape=(jax.ShapeDtypeStruct((r, T_SMALL * TN), BF16), jax.ShapeDtypeStruct((r, SMALL_W), F32)),
        grid=(r // tm, n // (2 * TN)),
        in_specs=_halo_specs(tm, d, r) + [
            pl.BlockSpec((1, 1, d), lambda i, j: (i // tpb, 0, 0)),
            pl.BlockSpec((1, 1, d), lambda i, j: (i // tpb, 0, 0)),
            pl.BlockSpec((1, d), lambda i, j: (0, 0)),
            pl.BlockSpec((1, d), lambda i, j: (0, 0)),
            pl.BlockSpec((d, 2 * TN), lambda i, j: (0, j)),
            pl.BlockSpec((1, 2 * TN), lambda i, j: (0, j)),
            pl.BlockSpec((LRU_CONV, 2 * TN), lambda i, j: (0, j)),
            pl.BlockSpec((1, 2 * TN), lambda i, j: (0, j)),
            pl.BlockSpec((tm, 128), lambda i, j: (i % tpb, 0)),
            pl.BlockSpec((tm, 128), lambda i, j: (i % tpb, 0)),
        ],
        out_specs=(pl.BlockSpec((tm, 2 * TN), lambda i, j: (i, jnp.minimum(j, T_SMALL // 2 - 1))),
                   pl.BlockSpec((tm, SMALL_W), lambda i, j: (i, 0))),
        scratch_shapes=[pltpu.VMEM((tm + 2 * HALO, d), BF16),
                        pltpu.VMEM((tm + 2 * HALO * len(_sub_blocks(tm)), TN), F32)],
        compiler_params=_cparams(("arbitrary", "arbitrary")),
        name="proj",
    )(xa, xa, xa, scl, shl, scc, shc, w, b, cw, cb, cos, sin)


def _bwd_index(i, n_ctx, n_tot):
    return jnp.where(i < n_ctx, n_ctx - 1 - i, n_tot + n_ctx - 1 - i)


def _tri_masks(c):
    row = lax.broadcasted_iota(jnp.int32, (c, c), 0)
    col = lax.broadcasted_iota(jnp.int32, (c, c), 1)
    incl = (row >= col, row <= col)
    strict = (row > col, row < col)
    return row, col, incl, strict, tuple(jnp.where(m, 1.0, 0.0).astype(BF16) for m in incl)


def _gla_step(qkf_ref, vf_ref, gf_ref, qkb_ref, vb_ref, gb_ref, wg_ref, bg_ref, of_ref, ob_ref, st_ref, nb):
    c = CHUNK
    hk = GLA_HEADS * GLA_DK
    _, _, incl, _, tri_bf = _tri_masks(c)
    refs = ((qkf_ref, vf_ref, gf_ref, of_ref), (qkb_ref, vb_ref, gb_ref, ob_ref))
    groups = [(b, d) for b in range(nb) for d in range(2)]
    z = [_mmb(refs[d][2][b, :, d * GLA_GATE_RANK:(d + 1) * GLA_GATE_RANK], wg_ref[d]) + bg_ref[d] for b, d in groups]
    la = [(jnp.minimum(v, 0.0) - jnp.log1p(jnp.exp(-jnp.abs(v)))) * (1.0 / GLA_GATE_NORM) for v in z]
    cum = [_cumdot(tri_bf[d], v) for (b, d), v in zip(groups, la)]
    last = [v[c - 1:c, :] if d == 0 else v[0:1, :] for (b, d), v in zip(groups, cum)]
    mid = [v[c // 2 - 1:c // 2, :] if d == 0 else v[c // 2:c // 2 + 1, :] for (b, d), v in zip(groups, cum)]
    q = [refs[d][0][b, :, :hk].astype(F32) for b, d in groups]
    k = [refs[d][0][b, :, hk:].astype(F32) for b, d in groups]
    vv = [refs[d][1][b].astype(BF16) for b, d in groups]
    n = range(len(groups))
    qd = [(q[g] * jnp.exp(cum[g] - mid[g])).astype(BF16) for g in n]
    km = [(k[g] * jnp.exp(mid[g] - cum[g])).astype(BF16) for g in n]
    kdec = [(k[g] * jnp.exp(last[g] - cum[g])).astype(BF16) for g in n]
    qs = [(q[g] * jnp.exp(cum[g])).astype(BF16) for g in n]
    dec = [jnp.exp(last[g]) for g in n]
    chains = [(g, h) for g in n for h in range(GLA_HEADS)]
    ksl = lambda h: slice(h * GLA_DK, (h + 1) * GLA_DK)
    vsl = lambda h: slice(h * GLA_DV, (h + 1) * GLA_DV)
    st = [st_ref[groups[g][0], groups[g][1], h] for g, h in chains]
    sc = [jnp.where(incl[groups[g][1]], _mm_nt(qd[g][:, ksl(h)], km[g][:, ksl(h)]), 0.0).astype(BF16)
          for g, h in chains]
    o_in = [_mm_nt(qs[g][:, ksl(h)], s.astype(BF16)) for (g, h), s in zip(chains, st)]
    o = [_mm(s, vv[g][:, vsl(h)]) + oi for (g, h), s, oi in zip(chains, sc, o_in)]
    ut = [_mm_tn(vv[g][:, vsl(h)], kdec[g][:, ksl(h)]) for g, h in chains]
    for (g, h), s, u in zip(chains, st, ut):
        st_ref[groups[g][0], groups[g][1], h] = s * dec[g][:, ksl(h)] + u
    for g, (b, d) in enumerate(groups):
        refs[d][3][b] = jnp.concatenate(o[g * GLA_HEADS:(g + 1) * GLA_HEADS], axis=1).astype(refs[d][3].dtype)


def _chunk_specs(nb, width, col, rowf):
    return pl.BlockSpec((nb, CHUNK, width), lambda i: (0, rowf(i), col))


def _lru_kernel(xf_ref, xb_ref, w_ref, gb_ref, lam_ref, of_ref, ob_ref, a_s, b_s, o_s, h_s, *, nb, tl):
    @pl.when(pl.program_id(0) == 0)
    def _():
        h_s[...] = jnp.zeros_like(h_s)

    chains = [(b, d) for b in range(nb) for d in range(2)]
    for n, (b, d) in enumerate(chains):
        xb = (xf_ref, xb_ref)[d][b]
        x = xb.astype(F32)
        gt = _mm(xb, w_ref[d]) + gb_ref[d]
        r = _sigmoid(gt[:, :LRU_WIDTH])
        ig = _sigmoid(gt[:, LRU_WIDTH:])
        log_a = (-LRU_C) * r * _softplus(-lam_ref[d])
        a = jnp.exp(log_a)
        a_s[n] = a
        b_s[n] = jnp.sqrt(1.0 - a * a) * ig * x

    def body(s, hs):
        out = []
        for n, (b, d) in enumerate(chains):
            rr = s if d == 0 else tl - 1 - s
            h = a_s[n, pl.ds(rr, 1), :] * hs[n] + b_s[n, pl.ds(rr, 1), :]
            o_s[n, pl.ds(rr, 1), :] = h
            out.append(h)
        return tuple(out)
    hs = lax.fori_loop(0, tl, body, tuple(h_s[n] for n in range(len(chains))), unroll=8)
    for n, (b, d) in enumerate(chains):
        h_s[n] = hs[n]
        (of_ref, ob_ref)[d][b] = o_s[n].astype(of_ref.dtype)


def _lru_call(p3, wbd, gb, lam, *, nb, tb, ctx_len, tl):
    ntl, nctl = tb // tl, ctx_len // tl
    bwd = lambda i: _bwd_index(i, nctl, ntl)
    out = jax.ShapeDtypeStruct((nb, tb, LRU_WIDTH), BF16)
    const = lambda a: pl.BlockSpec(a.shape, lambda i: (0,) * a.ndim)
    return pl.pallas_call(
        functools.partial(_lru_kernel, nb=nb, tl=tl),
        out_shape=(out, out),
        grid=(ntl,),
        in_specs=[pl.BlockSpec((nb, tl, TN), lambda i: (0, i, T_LRU_X)),
                  pl.BlockSpec((nb, tl, TN), lambda i: (0, bwd(i), T_LRU_X)),
                  const(wbd), const(gb), const(lam)],
        out_specs=(pl.BlockSpec((nb, tl, LRU_WIDTH), lambda i: (0, i, 0)),
                   pl.BlockSpec((nb, tl, LRU_WIDTH), lambda i: (0, bwd(i), 0))),
        scratch_shapes=[pltpu.VMEM((2 * nb, tl, LRU_WIDTH), F32)] * 3 + [pltpu.VMEM((2 * nb, 1, LRU_WIDTH), F32)],
        compiler_params=_cparams(("arbitrary",)),
        name="lru",
    )(p3, p3, wbd, gb, lam)


def _unit_tri_inverses(a_list, row, col):
    eye = jnp.where(row == col, 1.0, 0.0)
    blk = lambda s: (row // s) == (col // s)
    in8 = blk(8)
    a0 = [jnp.where(in8, a, 0.0) for a in a_list]
    p2 = [_mmb(x, x) for x in a0]
    ia = [eye - x for x in a0]
    t = [x + _mmb(x, p) for x, p in zip(ia, p2)]
    p4 = [_mmb(p, p) for p in p2]
    t = [x + _mmb(x, p) for x, p in zip(t, p4)]
    s = 8
    while s < CHUNK:
        sel = blk(2 * s) & jnp.logical_not(blk(s))
        y = [_mmb(jnp.where(sel, a, 0.0), x) for a, x in zip(a_list, t)]
        t = [x - _mmb(x, v) for x, v in zip(t, y)]
        s *= 2
    return t


def _gdn_step(qf_ref, kf_ref, vf_ref, gf_ref, qb_ref, kb_ref, vb_ref, gb_ref, alog_ref, dtb_ref,
              of_ref, ob_ref, s_ref, nb):
    c = CHUNK
    row, col, incl, strict, tri_bf = _tri_masks(c)
    refs = (((qf_ref, kf_ref, vf_ref), gf_ref, of_ref), ((qb_ref, kb_ref, vb_ref), gb_ref, ob_ref))
    neg_a = -jnp.exp(alog_ref[...])
    groups = [(b, d) for b in range(nb) for d in range(2)]
    gblk = [refs[d][1][b] for b, d in groups]
    g_all = [neg_a * _softplus(v + dtb_ref[...]) for v in gblk]
    beta_all = [_sigmoid(v) for v in gblk]
    gc_all = [_cumdot(tri_bf[d], v) for (b, d), v in zip(groups, g_all)]
    gc_t = [v.T for v in gc_all]

    chains = [(g, h) for g in range(len(groups)) for h in range(GDN_HEADS)]
    dirn = lambda g: groups[g][1]
    lane_a = lambda g, h: SM_A + dirn(g) * GDN_HEADS + h
    lane_b = lambda g, h: SM_B + dirn(g) * GDN_HEADS + h
    gcol = [gc_all[g][:, lane_a(g, h):lane_a(g, h) + 1] for g, h in chains]
    grow = [gc_t[g][lane_a(g, h):lane_a(g, h) + 1, :] for g, h in chains]
    beta = [beta_all[g][:, lane_b(g, h):lane_b(g, h) + 1] for g, h in chains]
    glast = [v[c - 1:c, :] if dirn(g) == 0 else v[0:1, :] for (g, h), v in zip(chains, gcol)]
    decay = [jnp.exp(jnp.where(incl[dirn(g)], gc - gr, -jnp.inf)) for (g, h), gc, gr in zip(chains, gcol, grow)]

    def head(g, h, part):
        b, d = groups[g]
        return refs[d][0][part][b, :, h * GDN_DK:(h + 1) * GDN_DK]
    k_bf = [head(g, h, 1) for g, h in chains]
    qh = [head(g, h, 0).astype(F32) for g, h in chains]
    kh = [k.astype(F32) for k in k_bf]
    vh = [head(g, h, 2).astype(F32) for g, h in chains]
    kb = [k * bt for k, bt in zip(kh, beta)]
    kq = [_mm_nt(jnp.concatenate([x.astype(BF16), y.astype(BF16)], axis=0), k) for x, y, k in zip(kb, qh, k_bf)]
    a = [jnp.where(strict[dirn(g)], x[:c] * dc, 0.0) for (g, h), x, dc in zip(chains, kq, decay)]
    attn = [jnp.where(incl[dirn(g)], x[c:] * dc, 0.0).astype(BF16) for (g, h), x, dc in zip(chains, kq, decay)]
    t = _unit_tri_inverses(a, row, col)
    egc = [jnp.exp(v) for v in gcol]
    sol = [_mmb(x, jnp.concatenate([v * bt, k * e], axis=1)) for x, v, bt, k, e in zip(t, vh, beta, kb, egc)]
    kdec = [(k * jnp.exp(gl - gc)).astype(BF16) for k, gl, gc in zip(kh, glast, gcol)]
    qdec = [(x * e).astype(BF16) for x, e in zip(qh, egc)]
    s = [s_ref[groups[g][0], groups[g][1], h] for g, h in chains]
    s_bf = [v.astype(BF16) for v in s]
    ws = [_mm(jnp.concatenate([x[:, GDN_DV:].astype(BF16), qd], axis=0), sb) for x, qd, sb in zip(sol, qdec, s_bf)]
    v_new = [x[:, :GDN_DV] - y[:c] for x, y in zip(sol, ws)]
    vn_bf = [v.astype(BF16) for v in v_new]
    o = [y[c:] + _mm(at, vn) for y, at, vn in zip(ws, attn, vn_bf)]
    for (g, h), sv, gl, kd, vn in zip(chains, s, glast, kdec, vn_bf):
        s_ref[groups[g][0], groups[g][1], h] = sv * jnp.exp(gl) + _mm_tn(kd, vn)
    for g, (b, d) in enumerate(groups):
        refs[d][2][b] = jnp.concatenate(o[g * GDN_HEADS:(g + 1) * GDN_HEADS], axis=1).astype(refs[d][2].dtype)


def _chunk_scan_kernel(gla_qkf, gla_vf, smallf, gla_qkb, gla_vb, smallb, wg_ref, bg_ref,
                       gdn_qf, gdn_kf, gdn_vf, gdn_qb, gdn_kb, gdn_vb, alog_ref, dtb_ref,
                       gla_of, gla_ob, gdn_of, gdn_ob, gla_state, gdn_state, *, nb):
    @pl.when(pl.program_id(0) == 0)
    def _():
        gla_state[...] = jnp.zeros_like(gla_state)
        gdn_state[...] = jnp.zeros_like(gdn_state)

    _gla_step(gla_qkf, gla_vf, smallf, gla_qkb, gla_vb, smallb, wg_ref, bg_ref, gla_of, gla_ob, gla_state, nb)
    _gdn_step(gdn_qf, gdn_kf, gdn_vf, smallf, gdn_qb, gdn_kb, gdn_vb, smallb, alog_ref, dtb_ref,
              gdn_of, gdn_ob, gdn_state, nb)


def _chunk_scan_call(p3, ps3, wg, bg, alog, dtb, *, nb, tb, ctx_len):
    nct, ncc = tb // CHUNK, ctx_len // CHUNK
    fwd = lambda i: i
    bwd = lambda i: _bwd_index(i, ncc, nct)
    width = GLA_HEADS * GLA_DV
    assert width == GDN_HEADS * GDN_DV
    tile = lambda t, rowf: _chunk_specs(nb, TN, t, rowf)
    small = lambda rowf: _chunk_specs(nb, SMALL_W, 0, rowf)
    const = lambda a: pl.BlockSpec(a.shape, lambda i: (0,) * a.ndim)
    gla_in = lambda rowf: [tile(T_GLA_QK, rowf), tile(T_GLA_V, rowf), small(rowf)]
    gdn_in = lambda rowf: [tile(t, rowf) for t in (T_GDN_Q, T_GDN_K, T_GDN_V)]
    out = jax.ShapeDtypeStruct((nb, tb, width), BF16)
    out_spec = lambda rowf: _chunk_specs(nb, width, 0, rowf)
    return pl.pallas_call(
        functools.partial(_chunk_scan_kernel, nb=nb),
        out_shape=(out, out, out, out),
        grid=(nct,),
        in_specs=gla_in(fwd) + gla_in(bwd) + [const(wg), const(bg)] + gdn_in(fwd) + gdn_in(bwd)
        + [const(alog), const(dtb)],
        out_specs=(out_spec(fwd), out_spec(bwd), out_spec(fwd), out_spec(bwd)),
        scratch_shapes=[pltpu.VMEM((nb, 2, GLA_HEADS, GLA_DV, GLA_DK), F32),
                        pltpu.VMEM((nb, 2, GDN_HEADS, GDN_DK, GDN_DV), F32)],
        compiler_params=_cparams(("arbitrary",)),
        name="chunk_scan",
    )(p3, p3, ps3, p3, p3, ps3, wg, bg, p3, p3, p3, p3, p3, p3, alog, dtb)


def _na_kernel(q_ref, k0_ref, k1_ref, k2_ref, v0_ref, v1_ref, v2_ref, kc_ref, vc_ref, bias_ref, o_ref,
               kbuf, vbuf, vcbuf, *, rows):
    jj = pl.program_id(1)
    tq = NA_ROWS * GRID_W
    hd = NA_HD

    @pl.when(jj == 0)
    def _():
        outs = []
        for h in range(NA_HEADS):
            sl = slice(h * hd, (h + 1) * hd)
            s = _mm_nt(q_ref[:, sl], k1_ref[:, sl])
            m = jnp.max(s, axis=-1, keepdims=True)
            e = jnp.exp(s - m)
            l = jnp.sum(e, axis=-1, keepdims=True)
            outs.append(_mm(e.astype(BF16), v1_ref[:, sl]) / l)
        o_ref[...] = jnp.concatenate(outs, axis=1).astype(o_ref.dtype)

    @pl.when(jj > 0)
    def _():
        jp = jj - 1
        pw = 2 * hd
        pairs = range(NA_HEADS // 2)
        ones = jnp.ones((tq, pw), BF16)
        for n, (kr, vr) in enumerate(((k0_ref, v0_ref), (k1_ref, v1_ref), (k2_ref, v2_ref))):
            kbuf[n * tq:(n + 1) * tq, :] = kr[...]
            for pr in pairs:
                vbuf[n * tq:(n + 1) * tq, 2 * pr * pw:(2 * pr + 1) * pw] = vr[:, pr * pw:(pr + 1) * pw]
                vbuf[n * tq:(n + 1) * tq, (2 * pr + 1) * pw:(2 * pr + 2) * pw] = ones
        for pr in pairs:
            vcbuf[:, 2 * pr * pw:(2 * pr + 1) * pw] = vc_ref[:, pr * pw:(pr + 1) * pw]
            vcbuf[:, (2 * pr + 1) * pw:(2 * pr + 2) * pw] = ones[:vcbuf.shape[0]]
        first = lax.broadcasted_iota(jnp.int32, (1, pw), 1) < hd
        def window(a):
            r = NA_ROWS * jp + a
            rs = jnp.clip(r - NA_WIN_R // 2, 0, rows - NA_WIN_R)
            start = pl.multiple_of((rs - NA_ROWS * (jp - 1)) * GRID_W, GRID_W)
            return pl.ds(start, NA_WIN_R * GRID_W), r - rs

        def scores(a):
            win, e_idx = window(a)
            qp = [q_ref[a * GRID_W:(a + 1) * GRID_W, pr * pw:(pr + 1) * pw] for pr in pairs]
            zero = jnp.zeros_like(qp[0])
            q2 = [jnp.concatenate([jnp.where(first, x, zero), jnp.where(first, zero, x)], axis=0) for x in qp]
            s_loc = [_mm_nt(q2[pr], kbuf[win, pr * pw:(pr + 1) * pw]) + bias_ref[e_idx, pr] for pr in pairs]
            s_ctx = [_mm_nt(q2[pr], kc_ref[:, pr * pw:(pr + 1) * pw]) for pr in pairs]
            return s_loc, s_ctx

        def softmax_numerators(s_loc, s_ctx):
            nk = s_ctx[0].shape[1]
            m = []
            for pr in pairs:
                mx = s_ctx[pr]
                for k0 in range(0, s_loc[pr].shape[1], nk):
                    mx = jnp.maximum(mx, s_loc[pr][:, k0:k0 + nk])
                m.append(jnp.max(mx, axis=-1, keepdims=True))
            return ([jnp.exp(s_loc[pr] - m[pr]).astype(BF16) for pr in pairs],
                    [jnp.exp(s_ctx[pr] - m[pr]).astype(BF16) for pr in pairs])

        def outputs(a, p_loc, p_ctx):
            win, _ = window(a)
            oe = [_mm(p_loc[pr], vbuf[win, 2 * pr * pw:(2 * pr + 2) * pw])
                  + _mm(p_ctx[pr], vcbuf[:, 2 * pr * pw:(2 * pr + 2) * pw]) for pr in pairs]
            on = [x[:, :pw] / x[:, pw:] for x in oe]
            o = [jnp.where(first, x[:GRID_W], x[GRID_W:]) for x in on]
            o_ref[a * GRID_W:(a + 1) * GRID_W, :] = jnp.concatenate(o, axis=1).astype(o_ref.dtype)

        sc = scores(0)
        for a in range(NA_ROWS):
            nxt = scores(a + 1) if a + 1 < NA_ROWS else None
            probs = softmax_numerators(*sc)
            outputs(a, *probs)
            sc = nxt


def _na_call(p, bias, *, nb, tb, ctx_len):
    r = p.shape[0]
    tq = NA_ROWS * GRID_W
    assert ctx_len == tq, "context block must be exactly one query block"
    nblk = tb // tq
    nlb = nblk - 1
    rows = (tb - ctx_len) // GRID_W
    assert rows >= NA_WIN_R and rows % NA_ROWS == 0
    width = NA_HEADS * NA_HD
    cur = lambda b, j: b * nblk + j
    prv = lambda b, j: b * nblk + 1 + jnp.clip(j - 2, 0, nlb - 1)
    nxt = lambda b, j: b * nblk + 1 + jnp.clip(j, 0, nlb - 1)
    blk = lambda rowf, t: pl.BlockSpec((tq, TN), lambda b, j: (rowf(b, j), t))
    return pl.pallas_call(
        functools.partial(_na_kernel, rows=rows),
        out_shape=jax.ShapeDtypeStruct((r, width), BF16),
        grid=(nb, nblk),
        in_specs=[blk(cur, T_NA_Q),
                  blk(prv, T_NA_K), blk(cur, T_NA_K), blk(nxt, T_NA_K),
                  blk(prv, T_NA_V), blk(cur, T_NA_V), blk(nxt, T_NA_V),
                  pl.BlockSpec((ctx_len, TN), lambda b, j: (b * nblk, T_NA_K)),
                  pl.BlockSpec((ctx_len, TN), lambda b, j: (b * nblk, T_NA_V)),
                  pl.BlockSpec(bias.shape, lambda b, j: (0, 0, 0, 0), pipeline_mode=pl.Buffered(1))],
        out_specs=pl.BlockSpec((tq, width), lambda b, j: (cur(b, j), 0)),
        scratch_shapes=[pltpu.VMEM((3 * tq, TN), BF16), pltpu.VMEM((3 * tq, 2 * TN), BF16),
                        pltpu.VMEM((ctx_len, 2 * TN), BF16)],
        compiler_params=_cparams(("arbitrary", "arbitrary")),
        name="na",
    )(p, p, p, p, p, p, p, p, p, bias)


def _na_bias_table(rpb):
    c = np.arange(GRID_W)
    cs = np.clip(c - NA_WIN_C // 2, 0, GRID_W - NA_WIN_C)
    kc = np.arange(GRID_W)
    inwin = (kc[None, :] >= cs[:, None]) & (kc[None, :] < cs[:, None] + NA_WIN_C)
    dc = np.clip(kc[None, :] - c[:, None] + NA_WIN_C - 1, 0, 2 * NA_WIN_C - 2)
    onehot = np.zeros((2 * NA_WIN_C - 1, GRID_W * GRID_W), np.float32)
    onehot[dc.reshape(-1), np.arange(GRID_W * GRID_W)] = 1.0
    byrow = jnp.einsum('hrd,dm->hrm', rpb, onehot, precision=lax.Precision.HIGHEST)
    byrow = jnp.where(inwin[None, None], byrow.reshape(NA_HEADS, 2 * NA_WIN_R - 1, GRID_W, GRID_W), NEG_BIG)
    tabs = [jnp.transpose(byrow[:, NA_WIN_R - 1 - e:2 * NA_WIN_R - 1 - e], (0, 2, 1, 3))
            .reshape(NA_HEADS, GRID_W, NA_WIN_R * GRID_W) for e in range(NA_WIN_R)]
    return jnp.stack(tabs).reshape(NA_WIN_R, NA_HEADS // 2, 2 * GRID_W, NA_WIN_R * GRID_W)


def _head_rms(o, g, width):
    outs = []
    for h in range(o.shape[1] // width):
        xs = o[:, h * width:(h + 1) * width]
        ms = jnp.mean(xs * xs, axis=-1, keepdims=True)
        outs.append(xs * lax.rsqrt(ms + NORM_EPS) * g)
    return jnp.concatenate(outs, axis=1)


def _merge_kernel(gof_ref, gob_ref, gg_ref, lhf_ref, lhb_ref, ly_ref, dof_ref, dob_ref, dz_ref, na_ref,
                  m0_ref, m1_ref, m2_ref, m3_ref, x_ref, g1l_ref, g1c_ref, wbr_ref, wout_ref, bout_ref,
                  gng_ref, gnd_ref, lng_ref, lnb_ref, o_ref, *, tm, tpb, ctx_len, alpha):
    p0 = (pl.program_id(0) % tpb) * tm
    for s0, ns in _sub_blocks(tm):
        rows = slice(s0, s0 + ns)
        f = lambda ref: ref[rows, :].astype(F32)
        y_a = _head_rms(f(gof_ref) + f(gob_ref), gng_ref[...], GLA_DV) * _silu(f(gg_ref))
        y_b = (f(lhf_ref) + f(lhb_ref)) * _gelu_tanh(f(ly_ref))
        y_c = _head_rms(f(dof_ref) + f(dob_ref), gnd_ref[...], GDN_DV) * _silu(f(dz_ref))
        y_d = na_ref[rows, :]
        m = None
        for n, (y, g_ref) in enumerate(((y_a, m0_ref), (y_b, m1_ref), (y_c, m2_ref), (y_d, m3_ref))):
            term = f(g_ref) * _mm(y.astype(BF16), wbr_ref[n])
            m = term if m is None else m + term
        out = _mm(m.astype(BF16), wout_ref[...]) + bout_ref[...]
        is_ctx = _row_pos(ns, p0 + s0) < ctx_len
        g1 = jnp.where(is_ctx, g1c_ref[...], g1l_ref[0])
        o_ref[rows, :] = _layer_norm(alpha * x_ref[rows, :] + g1 * out, lng_ref[...], lnb_ref[...])


def _merge_call(p, gof, gob, lhf, lhb, dof, dob, na, xa, g1l, g1c, wbr, wout, bout, gng, gnd, lng, lnb,
                *, tb, ctx_len, tm, alpha):
    r, d = xa.shape
    tpb = tb // tm
    bw = BRANCH_W
    row = lambda w: pl.BlockSpec((tm, w), lambda i: (i, 0))
    ptile = lambda t, w: pl.BlockSpec((tm, w), lambda i: (i, t * TN // w))
    const = lambda a: pl.BlockSpec(a.shape, lambda i: (0,) * a.ndim)
    kern = functools.partial(_merge_kernel, tm=tm, tpb=tpb, ctx_len=ctx_len, alpha=alpha)
    return pl.pallas_call(
        kern,
        out_shape=jax.ShapeDtypeStruct((r, d), F32),
        grid=(r // tm,),
        in_specs=[row(bw), row(bw), ptile(T_GLA_G, bw),
                  row(bw), row(bw),
                  ptile(T_LRU_Y, bw),
                  row(bw), row(bw), ptile(T_GDN_Z, bw),
                  row(bw),
                  pl.BlockSpec((tm, d), lambda i: (i, T_MERGE * TN // d + 0)),
                  pl.BlockSpec((tm, d), lambda i: (i, T_MERGE * TN // d + 1)),
                  pl.BlockSpec((tm, d), lambda i: (i, T_MERGE * TN // d + 2)),
                  pl.BlockSpec((tm, d), lambda i: (i, T_MERGE * TN // d + 3)),
                  row(d),
                  pl.BlockSpec((1, 1, d), lambda i: (i // tpb, 0, 0)),
                  const(g1c), const(wbr), const(wout), const(bout), const(gng), const(gnd), const(lng), const(lnb)],
        out_specs=row(d),
        compiler_params=_cparams(("arbitrary",)),
        name="merge",
    )(gof, gob, p, lhf, lhb, p, dof, dob, p, na, p, p, p, p, xa, g1l, g1c, wbr, wout, bout, gng, gnd, lng, lnb)


def _ffn_kernel(x_ref, xp_ref, xn_ref, scl_ref, shl_ref, gl_ref, scc_ref, shc_ref, gc_ref, wa_ref, wb_ref,
                ba_ref, bb_ref, cw_ref, cb_ref, wd_ref, bd_ref, lng_ref, lnb_ref, o_ref,
                lhs_ref, a_ref, h_ref, *, tm, tpb, ctx_len, tb, nc, tc, alpha):
    i = pl.program_id(0)
    c = pl.program_id(1)
    p0 = (i % tpb) * tm

    @pl.when(c == 0)
    def _():
        _build_lhs(lhs_ref, x_ref, xp_ref, xn_ref, scl_ref[0], shl_ref[0], scc_ref[...], shc_ref[...],
                   p0, tm, ctx_len)

    left = FFN_CONV // 2
    subs = _sub_blocks(tm)
    for s0, n in subs:
        a0 = _acc_row(s0, tm)
        a_ref[a0 - HALO:a0 + n + HALO, :] = _mm(lhs_ref[s0:s0 + n + 2 * HALO, :], wa_ref[...]) + ba_ref[...]
    gates = [_mm(lhs_ref[HALO + s0:HALO + s0 + n, :], wb_ref[...]) + bb_ref[...] for s0, n in subs]
    cw = cw_ref[...]
    hidden = [(_silu(_dwconv(a_ref, cw, left, _acc_row(s0, tm), n) + cb_ref[...]) * g).astype(BF16)
              for (s0, n), g in zip(subs, gates)]

    def gate_rows(r0):
        n = subs[0][1]
        return gates[r0 // n][r0 % n:r0 % n + HALO, :]
    fixes = [(r0, (_silu(_dwconv(a_ref, cw, left, _acc_row(r0, tm), HALO, p0 + r0, ctx_len, tb) + cb_ref[...])
                   * gate_rows(r0)).astype(BF16)) for r0 in _segment_end_windows(tm, ctx_len)]
    for cc in range(nc):
        @pl.when(c == cc)
        def _(cc=cc):
            for (s0, n), h in zip(subs, hidden):
                h_ref[s0:s0 + n, cc * tc:(cc + 1) * tc] = h
            for r0, fixed in fixes:
                h_ref[r0:r0 + HALO, cc * tc:(cc + 1) * tc] = fixed

    @pl.when(c == nc - 1)
    def _():
        for s0, n in subs:
            f = _mm(h_ref[s0:s0 + n, :], wd_ref[...]) + bd_ref[...]
            is_ctx = _row_pos(n, p0 + s0) < ctx_len
            g2 = jnp.where(is_ctx, gc_ref[...], gl_ref[0])
            o_ref[s0:s0 + n, :] = _layer_norm(alpha * x_ref[s0:s0 + n, :] + g2 * f, lng_ref[...], lnb_ref[...])


def _ffn_call(xa, scl, shl, gl, scc, shc, gc, w_up, b_up, cw, cb, w_down, b_down, lng, lnb,
              *, tb, ctx_len, tm, tc, alpha):
    r, d = xa.shape
    dff = w_down.shape[0]
    nc = dff // tc
    tpb = tb // tm
    lat = pl.BlockSpec((1, 1, d), lambda i, c: (i // tpb, 0, 0))
    vec = pl.BlockSpec((1, d), lambda i, c: (0, 0))
    kern = functools.partial(_ffn_kernel, tm=tm, tpb=tpb, ctx_len=ctx_len, tb=tb, nc=nc, tc=tc, alpha=alpha)
    return pl.pallas_call(
        kern,
        out_shape=jax.ShapeDtypeStruct((r, d), F32),
        grid=(r // tm, nc),
        in_specs=_halo_specs(tm, d, r) + [
            lat, lat, lat, vec, vec, vec,
            pl.BlockSpec((d, tc), lambda i, c: (0, c)),
            pl.BlockSpec((d, tc), lambda i, c: (0, nc + c)),
            pl.BlockSpec((1, tc), lambda i, c: (0, c)),
            pl.BlockSpec((1, tc), lambda i, c: (0, nc + c)),
            pl.BlockSpec((FFN_CONV, tc), lambda i, c: (0, c)),
            pl.BlockSpec((1, tc), lambda i, c: (0, c)),
            pl.BlockSpec((dff, d), lambda i, c: (0, 0), pipeline_mode=pl.Buffered(1)),
            vec, vec, vec],
        out_specs=pl.BlockSpec((tm, d), lambda i, c: (i, 0)),
        scratch_shapes=[pltpu.VMEM((tm + 2 * HALO, d), BF16),
                        pltpu.VMEM((tm + 2 * HALO * len(_sub_blocks(tm)), tc), F32),
                        pltpu.VMEM((tm, dff), BF16)],
        compiler_params=_cparams(("arbitrary", "arbitrary")),
        name="ffn",
    )(xa, xa, xa, scl, shl, gl, scc, shc, gc, w_up, w_up, b_up, b_up, cw, cb, w_down, b_down, lng, lnb)


def _pad_cols(a, width):
    return jnp.pad(a, ((0, 0), (0, width - a.shape[1])))


def _in_proj_params(w_in, b_in, lru_conv_w, lru_conv_b, gdn_conv_w):
    sizes = (GLA_HEADS * GLA_DK, GLA_HEADS * GLA_DK, GLA_HEADS * GLA_DV, GLA_HEADS * GLA_DV,
             GLA_GATE_RANK, GLA_GATE_RANK, LRU_WIDTH, LRU_WIDTH,
             GDN_HEADS * GDN_DK, GDN_HEADS * GDN_DK, GDN_HEADS * GDN_DV, GDN_HEADS * GDN_DV,
             2 * GDN_HEADS, 2 * GDN_HEADS, NA_HEADS * NA_HD, NA_HEADS * NA_HD, NA_HEADS * NA_HD)
    names = ('gla_q', 'gla_k', 'gla_v', 'gla_g', 'gla_fw', 'gla_bw', 'lru_x', 'lru_y', 'gdn_q', 'gdn_k',
             'gdn_v', 'gdn_z', 'gdn_a', 'gdn_b', 'na_q', 'na_k', 'na_v')
    offs = np.concatenate([[0], np.cumsum(sizes)])

    def reorder(m):
        col = {n: m[:, offs[k]:offs[k + 1]] for k, n in enumerate(names)}
        col['merge'] = m[:, offs[-1]:]
        small = _pad_cols(jnp.concatenate([col['gla_fw'], col['gla_bw'], col['gdn_a'], col['gdn_b']], axis=1), TN)
        tiles = {T_GLA_QK: jnp.concatenate([col['gla_q'] * (GLA_DK ** -0.5), col['gla_k']], axis=1),
                 T_GLA_V: col['gla_v'], T_GLA_G: col['gla_g'], T_GDN_Q: col['gdn_q'], T_GDN_K: col['gdn_k'],
                 T_GDN_V: col['gdn_v'], T_GDN_Z: col['gdn_z'], T_LRU_X: col['lru_x'], T_LRU_Y: col['lru_y'],
                 T_NA_Q: col['na_q'] * (NA_HD ** -0.5), T_NA_K: col['na_k'], T_NA_V: col['na_v'],
                 T_SMALL: _pad_cols(small, 2 * TN)}
        order = [tiles[t] for t in range(T_MERGE)] + [col['merge'], tiles[T_SMALL]]
        return jnp.concatenate(order, axis=1)
    w = reorder(w_in.astype(BF16))
    b = reorder(b_in[None, :])
    n = w.shape[1]
    assert n == N_TILES * TN
    cw = jnp.zeros((LRU_CONV, n), F32)
    cw = cw.at[:, T_LRU_X * TN:(T_LRU_X + 1) * TN].set(lru_conv_w)
    for k, t in enumerate((T_GDN_Q, T_GDN_K, T_GDN_V)):
        cw = cw.at[:, t * TN:(t + 1) * TN].set(gdn_conv_w[:, k * TN:(k + 1) * TN])
    cb = jnp.zeros((1, n), F32).at[0, T_LRU_X * TN:(T_LRU_X + 1) * TN].set(lru_conv_b)
    return w, b, cw, cb


def _rope_tables(ctx_len, seq_len):
    quarter = GLA_DK // 4
    t = jnp.arange(seq_len)
    inv = ROPE_BASE ** (-jnp.arange(quarter, dtype=F32) / quarter)
    ang = jnp.concatenate([(t // GRID_W).astype(F32)[:, None] * inv,
                           (t % GRID_W).astype(F32)[:, None] * inv], -1)
    cos = jnp.concatenate([jnp.ones((ctx_len, 2 * quarter), F32), jnp.cos(ang)], axis=0)
    sin = jnp.concatenate([jnp.zeros((ctx_len, 2 * quarter), F32), jnp.sin(ang)], axis=0)
    return jnp.concatenate([cos, cos, cos, cos], axis=1), jnp.concatenate([-sin, sin, -sin, sin], axis=1)


def _lru_gate_matrix(gate_w):
    eye = jnp.eye(LRU_BLOCKS, dtype=F32)
    dense = jnp.einsum('xgnde,nm->xndgme', gate_w, eye)
    return dense.reshape(2, LRU_WIDTH, 2 * LRU_WIDTH).astype(BF16)


def _small_lane_vec(v):
    return jnp.zeros((1, 128), F32).at[0, SM_A:SM_A + 2 * GDN_HEADS].set(v.reshape(-1))


def _row_tile(tb, target):
    best = HALO
    for t in range(HALO, target + 1, HALO):
        if tb % t == 0:
            best = t
    return best


def kernel(x, c, ctx, c_ctx, w_mod, b_mod, w_in, b_in, gla_w_gate, gla_b_gate, gla_norm, lru_conv_w, lru_conv_b,
           lru_gate_w, lru_gate_b, lru_lambda, gdn_conv_w, gdn_a_log, gdn_dt_bias, gdn_norm, na_rpb, w_branch,
           w_out, b_out, ln1_g, ln1_b, ffn_w_up, ffn_b_up, ffn_conv_w, ffn_conv_b, ffn_w_down, ffn_b_down,
           ln2_g, ln2_b):
    nb, seq_len, d = x.shape
    ctx_len = ctx.shape[1]
    depth = w_mod.shape[0]
    tb = ctx_len + seq_len
    alpha = (2 * depth) ** 0.25
    tm = _row_tile(tb, ROW_TILE)
    tl = NA_ROWS * GRID_W
    dims = dict(nb=nb, tb=tb, ctx_len=ctx_len)

    xa = jnp.concatenate([ctx, x], axis=1).reshape(nb * tb, d)
    crows = jnp.zeros((8, d), F32).at[:nb].set(c).at[nb].set(c_ctx)
    mods = _mod_call(crows, w_mod, b_mod)
    cos, sin = _rope_tables(ctx_len, seq_len)

    for l in range(depth):
        lat = [mods[l, :nb, k * d:(k + 1) * d].reshape(nb, 1, d) for k in range(6)]
        cx = [mods[l, nb:nb + 1, k * d:(k + 1) * d] for k in range(6)]
        w, b, cw, cb = _in_proj_params(w_in[l], b_in[l], lru_conv_w[l], lru_conv_b[l], gdn_conv_w[l])
        p, ps = _proj_call(xa, lat[1], lat[0], cx[1], cx[0], w, b, cw, cb, cos, sin, tm=tm, **dims)

        p3, ps3 = p.reshape(nb, tb, -1), ps.reshape(nb, tb, -1)
        flat = lambda a: a.reshape(nb * tb, -1)
        gof, gob, dof, dob = map(flat, _chunk_scan_call(
            p3, ps3, gla_w_gate[l], gla_b_gate[l].reshape(2, 1, -1),
            _small_lane_vec(gdn_a_log[l]), _small_lane_vec(gdn_dt_bias[l]), **dims))
        lhf, lhb = map(flat, _lru_call(p3, _lru_gate_matrix(lru_gate_w[l]), lru_gate_b[l].reshape(2, 1, -1),
                                       lru_lambda[l].reshape(2, 1, -1), tl=tl, **dims))
        na = _na_call(p, _na_bias_table(na_rpb[l]), **dims)

        xa = _merge_call(p, gof, gob, lhf, lhb, dof, dob, na, xa, lat[2], cx[2],
                         w_branch[l].astype(BF16), w_out[l].astype(BF16), b_out[l].reshape(1, d),
                         gla_norm[l].reshape(1, -1),
                         gdn_norm[l].reshape(1, -1), ln1_g[l].reshape(1, d), ln1_b[l].reshape(1, d),
                         tb=tb, ctx_len=ctx_len, tm=_row_tile(tb, MERGE_TILE), alpha=alpha)
        xa = _ffn_call(xa, lat[4], lat[3], lat[5], cx[4], cx[3], cx[5],
                       ffn_w_up[l].astype(BF16), ffn_b_up[l].reshape(1, -1), ffn_conv_w[l],
                       ffn_conv_b[l].reshape(1, -1), ffn_w_down[l].astype(BF16), ffn_b_down[l].reshape(1, d),
                       ln2_g[l].reshape(1, d), ln2_b[l].reshape(1, d),
                       tb=tb, ctx_len=ctx_len, tm=tm, tc=256, alpha=alpha)
    return xa.reshape(nb, tb, d)[:, ctx_len:, :]
```
